```python
import jax
import jax.numpy as jnp
from jax import lax
import numpy as np


D_MODEL = 1024
BATCH = 2
SEQ = 8192
DEPTH = 4

CTX_LEN = 256
GRID_W = 64
D_CONV = 512
CONV_W = 31
DN_HEADS = 4
DN_HEAD_DIM = 128
D_DN = DN_HEADS * DN_HEAD_DIM
D_MIX = D_CONV + D_DN
SHORT_CONV_W = 5
CHUNK = 64
EPS = 1e-6
LN_EPS = 1e-5
QKV_START = 3 * D_CONV
SPLITS = (D_CONV, 2 * D_CONV, 3 * D_CONV, 3 * D_CONV + 3 * D_DN, 3 * D_CONV + 4 * D_DN,
          3 * D_CONV + 4 * D_DN + 2 * DN_HEADS)
REST_SPLITS = (3 * D_DN, 4 * D_DN, 4 * D_DN + 2 * DN_HEADS)
D_IN = 3 * D_CONV + 4 * D_DN + 4 * DN_HEADS

kernel_name = 'hymba_conformer_gated_deltanet_dit'


def rms_norm(x, w):
    xf = x.astype(jnp.float32)
    y = xf * lax.rsqrt(jnp.mean(xf * xf, axis=-1, keepdims=True) + EPS)
    return (y * w.astype(jnp.float32)).astype(x.dtype)


def layer_norm(x, w, b):
    xf = x.astype(jnp.float32)
    mu = jnp.mean(xf, axis=-1, keepdims=True)
    var = jnp.mean(jnp.square(xf - mu), axis=-1, keepdims=True)
    y = (xf - mu) * lax.rsqrt(var + LN_EPS)
    return (y * w.astype(jnp.float32) + b.astype(jnp.float32)).astype(x.dtype)


def l2norm(x):
    return x * lax.rsqrt(jnp.sum(x * x, axis=-1, keepdims=True) + EPS)


def dwconv1d(u, w):
    k, ch = w.shape
    pad = (k - 1) // 2
    return lax.conv_general_dilated(u, w.astype(u.dtype)[:, None, :], window_strides=(1,),
                                    padding=[(pad, pad)], dimension_numbers=('NWC', 'WIO', 'NWC'),
                                    feature_group_count=ch)


def seq_dwconv(u, w, b):
    return dwconv1d(u, w) + b.astype(u.dtype)


def grid_dwconv(u, w, b):
    bsz, t, ch = u.shape
    rows = t // GRID_W
    half = ch // 2
    g = u.reshape(bsz, rows, GRID_W, ch)
    uh = g[..., :half].reshape(bsz * rows, GRID_W, half)
    yh = dwconv1d(uh, w[:, :half]).reshape(bsz, rows, GRID_W, half)
    uv = g[..., half:].transpose(0, 2, 1, 3).reshape(bsz * GRID_W, rows, half)
    yv = dwconv1d(uv, w[:, half:]).reshape(bsz, GRID_W, rows, half).transpose(0, 2, 1, 3)
    return jnp.concatenate([yh, yv], axis=-1).reshape(bsz, t, ch) + b.astype(u.dtype)


def conv_branch(ga, gb, cgate, conv_w, conv_b, ln_w, ln_b, conv_fn):
    u = ga * jax.nn.sigmoid(gb)
    u = conv_fn(u, conv_w, conv_b)
    u = jax.nn.silu(layer_norm(u, ln_w, ln_b))
    return u * jax.nn.silu(cgate)


def chunk_gated_delta(q, k, v, g, beta, s0):
    bsz, h, t, dk = k.shape
    dv = v.shape[-1]
    n = t // CHUNK
    q = q * (dk ** -0.5)
    q, k, v = (m.reshape(bsz, h, n, CHUNK, m.shape[-1]) for m in (q, k, v))
    g = jnp.cumsum(g.reshape(bsz, h, n, CHUNK), axis=-1)
    beta = beta.reshape(bsz, h, n, CHUNK)[..., None]
    incl = jnp.tril(jnp.ones((CHUNK, CHUNK), dtype=bool))
    strict = jnp.tril(jnp.ones((CHUNK, CHUNK), dtype=bool), -1)
    diff = g[..., :, None] - g[..., None, :]
    decay = jnp.where(incl, jnp.exp(jnp.where(incl, diff, 0.0)), 0.0)
    kb = k * beta
    lmat = jnp.where(strict, jnp.einsum('bhncd,bhnsd->bhncs', kb, k) * decay, 0.0)
    eye = jnp.eye(CHUNK, dtype=jnp.float32)
    rhs = jnp.concatenate([v * beta, kb * jnp.exp(g)[..., None]], axis=-1)
    sol = lax.linalg.triangular_solve(eye + lmat, rhs, left_side=True, lower=True,
                                      unit_diagonal=True)
    u, w = sol[..., :dv], sol[..., dv:]
    attn = jnp.where(incl, jnp.einsum('bhncd,bhnsd->bhncs', q, k) * decay, 0.0)
    qg = q * jnp.exp(g)[..., None]
    kt = k * jnp.exp(g[..., -1:] - g)[..., None]
    glast = jnp.exp(g[..., -1])[..., None, None]

    def step(state, xs):
        attn_i, qg_i, kt_i, u_i, w_i, gl_i = xs
        v_new = u_i - jnp.einsum('bhcd,bhde->bhce', w_i, state)
        o_i = jnp.einsum('bhcd,bhde->bhce', qg_i, state) + jnp.einsum('bhcs,bhse->bhce', attn_i, v_new)
        state = state * gl_i + jnp.einsum('bhcd,bhce->bhde', kt_i, v_new)
        return state, o_i

    xs = tuple(jnp.moveaxis(m, 2, 0) for m in (attn, qg, kt, u, w, glast))
    state, o = lax.scan(step, s0, xs)
    o = jnp.moveaxis(o, 0, 2).reshape(bsz, h, t, dv)
    return o, state


def delta_branch(qkv, z, a, bt, sconv_w, a_log, dt_bias, norm_w, s0_f, s0_b):
    bsz, t, _ = qkv.shape
    qkv = jax.nn.silu(dwconv1d(qkv, sconv_w))
    heads = lambda m: m.reshape(bsz, t, DN_HEADS, DN_HEAD_DIM).transpose(0, 2, 1, 3).astype(jnp.float32)
    q, k, v = (heads(m) for m in jnp.split(qkv, 3, axis=-1))
    q, k = l2norm(q), l2norm(k)
    a = a.astype(jnp.float32).reshape(bsz, t, 2, DN_HEADS).transpose(2, 0, 3, 1)
    g = -jnp.exp(a_log.astype(jnp.float32))[:, None, :, None] * jax.nn.softplus(
        a + dt_bias.astype(jnp.float32)[:, None, :, None])
    beta = jax.nn.sigmoid(bt.astype(jnp.float32).reshape(bsz, t, 2, DN_HEADS).transpose(2, 0, 3, 1))
    o_f, s_f = chunk_gated_delta(q, k, v, g[0], beta[0], s0_f)
    flip = lambda m: jnp.flip(m, axis=2)
    o_b, s_b = chunk_gated_delta(flip(q), flip(k), flip(v), flip(g[1]), flip(beta[1]), s0_b)
    o = (o_f + flip(o_b)).transpose(0, 2, 1, 3)
    zg = jax.nn.silu(z.astype(jnp.float32).reshape(bsz, t, DN_HEADS, DN_HEAD_DIM))
    o = rms_norm(o, norm_w) * zg
    return o.reshape(bsz, t, D_DN).astype(z.dtype), s_f, s_b


def modulate(x, cond, norm_w, w_ada, b_ada):
    mod = jax.nn.silu(cond) @ w_ada + b_ada
    shift, scale, gate = (jnp.expand_dims(m, -2) for m in jnp.split(mod, 3, axis=-1))
    return rms_norm(x, norm_w) * (1 + scale) + shift, gate


def mixer_sublayer(x, cond, norm_w, w_ada, b_ada, w_in, conv_w, conv_b, cln_w, cln_b, sconv_w,
                   a_log, dt_bias, dn_norm_w, w_out, conv_fn, s0_f, s0_b):
    h, gate = modulate(x, cond, norm_w, w_ada, b_ada)
    proj = h @ w_in
    ga, gb, cgate, qkv, z, a, bt = jnp.split(proj, SPLITS, axis=-1)
    y_conv = conv_branch(ga, gb, cgate, conv_w, conv_b, cln_w, cln_b, conv_fn)
    y_dn, s_f, s_b = delta_branch(qkv, z, a, bt, sconv_w, a_log, dt_bias, dn_norm_w, s0_f, s0_b)
    out = jnp.concatenate([y_conv, y_dn], axis=-1) @ w_out
    return x + gate * out, s_f, s_b


def context_states(x, cond, norm_w, w_ada, b_ada, w_in, sconv_w, a_log, dt_bias, dn_norm_w, s0_f, s0_b):
    h, _ = modulate(x, cond, norm_w, w_ada, b_ada)
    qkv, z, a, bt = jnp.split(h @ w_in[:, QKV_START:], REST_SPLITS, axis=-1)
    _, s_f, s_b = delta_branch(qkv, z, a, bt, sconv_w, a_log, dt_bias, dn_norm_w, s0_f, s0_b)
    return s_f, s_b


def setup_inputs(seed: int = 0) -> dict:
    key = jax.random.key(seed)
    ks = jax.random.split(key, 20)
    f32 = jnp.float32
    nrm = lambda k, shape, s: jax.random.normal(k, shape, f32) * s
    a_init = jnp.log(jax.random.uniform(ks[14], (DEPTH, 2, DN_HEADS), f32, 1.0, 16.0))
    dt = jnp.exp(jax.random.uniform(ks[15], (DEPTH, 2, DN_HEADS), f32, math_log(0.001), math_log(0.1)))
    dt_bias = dt + jnp.log(-jnp.expm1(-dt))
    return {
        'x': nrm(ks[0], (BATCH, SEQ, D_MODEL), 1.0),
        'c': nrm(ks[1], (BATCH, D_MODEL), 1.0),
        'ctx': nrm(ks[2], (BATCH, CTX_LEN, D_MODEL), 1.0),
        'c_ctx': nrm(ks[3], (D_MODEL,), 1.0),
        'norm_w': 1.0 + nrm(ks[4], (DEPTH, D_MODEL), 0.1),
        'w_ada': nrm(ks[5], (DEPTH, D_MODEL, 3 * D_MODEL), 0.5 * D_MODEL ** -0.5),
        'b_ada': nrm(ks[6], (DEPTH, 3 * D_MODEL), 0.02),
        'w_in': nrm(ks[7], (DEPTH, D_MODEL, D_IN), D_MODEL ** -0.5),
        'conv_w': nrm(ks[8], (DEPTH, CONV_W, D_CONV), CONV_W ** -0.5),
        'conv_b': nrm(ks[9], (DEPTH, D_CONV), 0.02),
        'conv_ln_w': 1.0 + nrm(ks[10], (DEPTH, D_CONV), 0.1),
        'conv_ln_b': nrm(ks[11], (DEPTH, D_CONV), 0.02),
        'short_conv_w': nrm(ks[12], (DEPTH, SHORT_CONV_W, 3 * D_DN), SHORT_CONV_W ** -0.5),
        'a_log': a_init,
        'dt_bias': dt_bias,
        'dn_norm_w': 1.0 + nrm(ks[16], (DEPTH, DN_HEAD_DIM), 0.1),
        'w_out': nrm(ks[17], (DEPTH, D_MIX, D_MODEL), D_MIX ** -0.5),
        'final_norm_w': 1.0 + nrm(ks[18], (D_MODEL,), 0.1),
    }


def math_log(v):
    return float(np.log(v))


def reference(x, c, ctx, c_ctx, norm_w, w_ada, b_ada, w_in, conv_w, conv_b, conv_ln_w, conv_ln_b,
              short_conv_w, a_log, dt_bias, dn_norm_w, w_out, final_norm_w):
    bsz = x.shape[0]
    zeros = jnp.zeros((bsz, DN_HEADS, DN_HEAD_DIM, DN_HEAD_DIM), jnp.float32)
    for l in range(DEPTH):
        if l < DEPTH - 1:
            ctx_new, s_f, s_b = mixer_sublayer(
                ctx, c_ctx, norm_w[l], w_ada[l], b_ada[l], w_in[l], conv_w[l], conv_b[l],
                conv_ln_w[l], conv_ln_b[l], short_conv_w[l], a_log[l], dt_bias[l], dn_norm_w[l],
                w_out[l], seq_dwconv, zeros, zeros)
        else:
            ctx_new = ctx
            s_f, s_b = context_states(
                ctx, c_ctx, norm_w[l], w_ada[l], b_ada[l], w_in[l], short_conv_w[l], a_log[l],
                dt_bias[l], dn_norm_w[l], zeros, zeros)
        x, _, _ = mixer_sublayer(
            x, c, norm_w[l], w_ada[l], b_ada[l], w_in[l], conv_w[l], conv_b[l], conv_ln_w[l],
            conv_ln_b[l], short_conv_w[l], a_log[l], dt_bias[l], dn_norm_w[l], w_out[l],
            grid_dwconv, s_f, s_b)
        ctx = ctx_new
    return rms_norm(x, final_norm_w)
```

```python
import functools

import jax
import jax.numpy as jnp
from jax import lax
from jax.experimental import pallas as pl
from jax.experimental.pallas import tpu as pltpu

D_MODEL = 1024
DEPTH = 4
GRID_W = 64
D_CONV = 512
CONV_W = 31
CONV_PAD = (CONV_W - 1) // 2
DN_HEADS = 4
DN_HEAD_DIM = 128
D_DN = DN_HEADS * DN_HEAD_DIM
SHORT_CONV_W = 5
SHORT_PAD = (SHORT_CONV_W - 1) // 2
CHUNK = 64
EPS = 1e-6
LN_EPS = 1e-5

LANES = 128
SUBLANES = 8
DELTA_BLOCK = 256
CHUNKS_PER_BLOCK = DELTA_BLOCK // CHUNK
SEG_GAP = 16
VMEM_LIMIT = 48 * 1024 * 1024

F32 = jnp.float32
BF16 = jnp.bfloat16


def _dot(a, b):
    return jnp.dot(a, b, preferred_element_type=F32)


def _dot_nt(a, b):
    return lax.dot_general(a, b, (((1,), (1,)), ((), ())), preferred_element_type=F32)


def _sigmoid(x):
    return 1.0 / (1.0 + jnp.exp(-x))


def _silu(x):
    return x * _sigmoid(x)


def _softplus(x):
    return jnp.maximum(x, 0.0) + jnp.log1p(jnp.exp(-jnp.abs(x)))


def _dot_mask_f32(mask_bf16, x):
    hi = x.astype(BF16)
    r1 = x - hi.astype(F32)
    mid = r1.astype(BF16)
    lo = (r1 - mid.astype(F32)).astype(BF16)
    return _dot(mask_bf16, hi) + _dot(mask_bf16, mid) + _dot(mask_bf16, lo)


def _ada_kernel(cond_ref, w_ref, b_ref, o_ref):
    s = _silu(cond_ref[...])
    o_ref[0] = jnp.dot(s, w_ref[0], precision=lax.Precision.HIGHEST,
                       preferred_element_type=F32) + b_ref[0]


def _ada_all(cond_rows, w_ada, b_ada):
    tn = 1024
    return pl.pallas_call(
        _ada_kernel,
        grid=(DEPTH, 3 * D_MODEL // tn),
        in_specs=[
            pl.BlockSpec((SUBLANES, D_MODEL), lambda l, j: (0, 0)),
            pl.BlockSpec((1, D_MODEL, tn), lambda l, j: (l, 0, j)),
            pl.BlockSpec((1, 1, tn), lambda l, j: (l, 0, j)),
        ],
        out_specs=pl.BlockSpec((1, SUBLANES, tn), lambda l, j: (l, 0, j)),
        out_shape=jax.ShapeDtypeStruct((DEPTH, SUBLANES, 3 * D_MODEL), F32),
        compiler_params=pltpu.CompilerParams(vmem_limit_bytes=VMEM_LIMIT),
        name="ada",
    )(cond_rows, w_ada, b_ada.reshape(DEPTH, 1, 3 * D_MODEL))


def _inproj_kernel(*refs, full):
    if full:
        (x_ref, nw_ref, shift_ref, scale_ref, wa_ref, wb_ref, wcg_ref, wz_ref, wqkv_ref, wab_ref,
         alog_ref, dtb_ref, u_ref, cgs_ref, zs_ref, qkv_ref, gb_ref) = refs
    else:
        (x_ref, nw_ref, shift_ref, scale_ref, wqkv_ref, wab_ref, alog_ref, dtb_ref,
         qkv_ref, gb_ref) = refs
    x = x_ref[0]
    y = x * lax.rsqrt(jnp.mean(x * x, axis=-1, keepdims=True) + EPS) * nw_ref[...]
    h = (y * (1.0 + scale_ref[0]) + shift_ref[0]).astype(BF16)
    if full:
        u_ref[0] = _dot(h, wa_ref[...]) * _sigmoid(_dot(h, wb_ref[...]))
        cgs_ref[0] = _silu(_dot(h, wcg_ref[...]))
        zs_ref[0] = _silu(_dot(h, wz_ref[...]))
    qkv_ref[0] = _dot(h, wqkv_ref[...])
    ab = _dot(h, wab_ref[...])
    g = -jnp.exp(alog_ref[...]) * _softplus(ab + dtb_ref[...])
    lane = lax.broadcasted_iota(jnp.int32, ab.shape, 1)
    gb_ref[0] = jnp.where(lane < 2 * DN_HEADS, g, _sigmoid(ab))


def _inproj(x, nw, shift, scale, wts, alog, dtb, *, full, tm):
    bsz, t, _ = x.shape
    row = lambda n: pl.BlockSpec((1, tm, n), lambda b, i: (b, i, 0))
    const = lambda shp: pl.BlockSpec(shp, lambda b, i: (0,) * len(shp))
    per_b = pl.BlockSpec((1, 1, D_MODEL), lambda b, i: (b, 0, 0))
    w_specs = [const(w.shape) for w in wts]
    in_specs = [row(D_MODEL), const((1, D_MODEL)), per_b, per_b] + w_specs + [const((1, LANES))] * 2
    widths = ([D_CONV, D_CONV, D_DN] if full else []) + [3 * D_DN, LANES]
    return pl.pallas_call(
        functools.partial(_inproj_kernel, full=full),
        grid=(bsz, t // tm),
        in_specs=in_specs,
        out_specs=[row(n) for n in widths],
        out_shape=[jax.ShapeDtypeStruct((bsz, t, n), F32) for n in widths],
        compiler_params=pltpu.CompilerParams(
            dimension_semantics=("parallel", "parallel"), vmem_limit_bytes=VMEM_LIMIT),
        name="inproj_full" if full else "inproj_dn",
    )(x, nw, shift, scale, *wts, alog, dtb)


def _conv_segments(u_ref, w_ref, o_ref, pad_ref, *, seg, nseg):
    stride = seg + SEG_GAP
    zeros_gap = jnp.zeros((SEG_GAP, LANES), F32)

    def fill(r, carry):
        base = pl.multiple_of(r * stride, SUBLANES)
        src = pl.multiple_of(r * seg, SUBLANES)
        pad_ref[pl.ds(base, SEG_GAP), :] = zeros_gap
        pad_ref[pl.ds(base + SEG_GAP, seg), :] = u_ref[0, pl.ds(src, seg), :]
        return carry

    lax.fori_loop(0, nseg, fill, 0)
    pad_ref[pl.ds(nseg * stride, SEG_GAP), :] = zeros_gap

    def body(r, carry):
        base = pl.multiple_of(r * stride, SUBLANES)
        dst = pl.multiple_of(r * seg, SUBLANES)
        acc = jnp.zeros((seg, LANES), F32)
        for k in range(CONV_W):
            acc = acc + pad_ref[pl.ds(base + (SEG_GAP - CONV_PAD) + k, seg), :] * w_ref[k:k + 1, :]
        o_ref[0, pl.ds(dst, seg), :] = acc
        return carry

    lax.fori_loop(0, nseg, body, 0)


def _conv_strided(u_ref, w_ref, o_ref, pad_ref, *, t, step):
    halo = CONV_PAD * step
    pad_ref[pl.ds(0, halo), :] = jnp.zeros((halo, LANES), F32)
    pad_ref[pl.ds(halo + t, halo), :] = jnp.zeros((halo, LANES), F32)
    pad_ref[pl.ds(halo, t), :] = u_ref[0]

    def body(r, carry):
        base = pl.multiple_of(r * step, SUBLANES)
        acc = jnp.zeros((step, LANES), F32)
        for k in range(CONV_W):
            acc = acc + pad_ref[pl.ds(base + k * step, step), :] * w_ref[k:k + 1, :]
        o_ref[0, pl.ds(base, step), :] = acc
        return carry

    lax.fori_loop(0, t // step, body, 0)


def _grid_conv_kernel(u_ref, w_ref, o_ref, pad_ref, *, t):
    j = pl.program_id(1)
    half_blocks = D_CONV // LANES // 2

    @pl.when(j < half_blocks)
    def _():
        _conv_segments(u_ref, w_ref, o_ref, pad_ref, seg=GRID_W, nseg=t // GRID_W)

    @pl.when(j >= half_blocks)
    def _():
        _conv_strided(u_ref, w_ref, o_ref, pad_ref, t=t, step=GRID_W)


def _seq_conv_kernel(u_ref, w_ref, o_ref, pad_ref, *, t):
    _conv_segments(u_ref, w_ref, o_ref, pad_ref, seg=t, nseg=1)


def _dwconv(u, w_pad, *, grid_mode):
    bsz, t, _ = u.shape
    if grid_mode:
        rows = t // GRID_W
        pad_rows = max(rows * (GRID_W + SEG_GAP) + SEG_GAP, t + 2 * CONV_PAD * GRID_W)
        body = functools.partial(_grid_conv_kernel, t=t)
    else:
        pad_rows = t + 2 * SEG_GAP
        body = functools.partial(_seq_conv_kernel, t=t)
    return pl.pallas_call(
        body,
        grid=(bsz, D_CONV // LANES),
        in_specs=[
            pl.BlockSpec((1, t, LANES), lambda b, j: (b, 0, j)),
            pl.BlockSpec((4 * SUBLANES, LANES), lambda b, j: (0, j)),
        ],
        out_specs=pl.BlockSpec((1, t, LANES), lambda b, j: (b, 0, j)),
        out_shape=jax.ShapeDtypeStruct((bsz, t, D_CONV), F32),
        scratch_shapes=[pltpu.VMEM((pad_rows, LANES), F32)],
        compiler_params=pltpu.CompilerParams(
            dimension_semantics=("parallel", "parallel"), vmem_limit_bytes=VMEM_LIMIT),
        name="grid_conv" if grid_mode else "seq_conv",
    )(u, w_pad)


def _delta_kernel(prev_ref, cur_ref, next_ref, gb_ref, scw_ref, s0_ref, o_ref, sfin_ref,
                  xpad, s_scr, u_scr, wq_scr, attn_scr, kt_scr, *, reverse, nblk):
    nb = DELTA_BLOCK
    i = pl.program_id(1)
    blk = (nblk - 1 - i) if reverse else i

    @pl.when(i == 0)
    def _():
        s_scr[...] = s0_ref[0]

    xpad[pl.ds(0, SUBLANES), :] = jnp.where(blk > 0, prev_ref[0], 0.0)
    xpad[pl.ds(SUBLANES, nb), :] = cur_ref[0]
    xpad[pl.ds(SUBLANES + nb, SUBLANES), :] = jnp.where(blk < nblk - 1, next_ref[0], 0.0)

    def conv_silu(col):
        acc = jnp.zeros((nb, LANES), F32)
        for j in range(SHORT_CONV_W):
            acc = acc + (xpad[pl.ds(SUBLANES - SHORT_PAD + j, nb), col:col + LANES]
                         * scw_ref[j:j + 1, col:col + LANES])
        return _silu(acc)

    def l2n(m):
        return m * lax.rsqrt(jnp.sum(m * m, axis=-1, keepdims=True) + EPS)

    row = lax.broadcasted_iota(jnp.int32, (nb, nb), 0)
    col = lax.broadcasted_iota(jnp.int32, (nb, nb), 1)
    same = (row // CHUNK) == (col // CHUNK)
    if reverse:
        incl = jnp.logical_and(same, col >= row)
        strict = jnp.logical_and(same, col > row)
    else:
        incl = jnp.logical_and(same, col <= row)
        strict = jnp.logical_and(same, col < row)

    gbv = gb_ref[0]
    gc = _dot_mask_f32(jnp.where(incl, 1.0, 0.0).astype(BF16), gbv)
    tot = _dot_mask_f32(jnp.where(same, 1.0, 0.0).astype(BF16), gbv)
    gc_t = gc.T

    gl = []
    for h in range(DN_HEADS):
        lg = (DN_HEADS if reverse else 0) + h
        lb = 2 * DN_HEADS + lg
        q = l2n(conv_silu(h * LANES)) * (DN_HEAD_DIM ** -0.5)
        k = l2n(conv_silu(D_DN + h * LANES))
        v = conv_silu(2 * D_DN + h * LANES)
        gcol = gc[:, lg:lg + 1]
        grow = gc_t[lg:lg + 1, :]
        tcol = tot[:, lg:lg + 1]
        beta = gbv[:, lb:lb + 1]
        decay = jnp.where(incl, jnp.exp(jnp.where(incl, gcol - grow, 0.0)), 0.0)
        kb = k * beta
        kbf = k.astype(BF16)
        lmat = jnp.where(strict, _dot_nt(kb.astype(BF16), kbf) * decay, 0.0)
        lb16 = lmat.astype(BF16)
        n = -lmat
        p = _dot(lb16, lb16)
        for it in range(5):
            pb = p.astype(BF16)
            n = n + p + _dot(n.astype(BF16), pb)
            if it < 4:
                p = _dot(pb, pb)
        eg = jnp.exp(gcol)
        rhs = jnp.concatenate([v * beta, kb * eg], axis=-1)
        sol = rhs + _dot(n.astype(BF16), rhs.astype(BF16))
        attn = jnp.where(incl, _dot_nt(q.astype(BF16), kbf) * decay, 0.0)
        kt_t = (k * jnp.exp(tcol - gcol)).T
        qg = q * eg
        u_scr[h] = sol[:, :LANES]
        for c in range(CHUNKS_PER_BLOCK):
            sl = slice(c * CHUNK, (c + 1) * CHUNK)
            wq_scr[h, c, 0:CHUNK, :] = sol[sl, LANES:].astype(BF16)
            wq_scr[h, c, CHUNK:2 * CHUNK, :] = qg[sl, :].astype(BF16)
            attn_scr[h, c] = attn[sl, sl].astype(BF16)
            kt_scr[h, c] = kt_t[:, sl].astype(BF16)
        gl.append([jnp.exp(tot[c * CHUNK:c * CHUNK + 1, lg:lg + 1]) for c in range(CHUNKS_PER_BLOCK)])

    for step in range(CHUNKS_PER_BLOCK):
        c = (CHUNKS_PER_BLOCK - 1 - step) if reverse else step
        for h in range(DN_HEADS):
            s = s_scr[h]
            r = _dot(wq_scr[h, c], s.astype(BF16))
            v_new = (u_scr[h, pl.ds(c * CHUNK, CHUNK), :] - r[:CHUNK]).astype(BF16)
            o_ref[0, pl.ds(c * CHUNK, CHUNK), h * LANES:(h + 1) * LANES] = (
                r[CHUNK:] + _dot(attn_scr[h, c], v_new))
            s_scr[h] = s * gl[h][c] + _dot(kt_scr[h, c], v_new)

    @pl.when(i == nblk - 1)
    def _():
        sfin_ref[0] = s_scr[...]


def _delta(qkv, gb, scw, s0, *, reverse):
    bsz, t, _ = qkv.shape
    nb = DELTA_BLOCK
    nblk = t // nb
    sub_per_blk = nb // SUBLANES
    last_sub = t // SUBLANES - 1
    if reverse:
        bidx = lambda i: nblk - 1 - i
    else:
        bidx = lambda i: i
    state_spec = pl.BlockSpec((1, DN_HEADS, DN_HEAD_DIM, DN_HEAD_DIM), lambda b, i: (b, 0, 0, 0))
    return pl.pallas_call(
        functools.partial(_delta_kernel, reverse=reverse, nblk=nblk),
        grid=(bsz, nblk),
        in_specs=[
            pl.BlockSpec((1, SUBLANES, 3 * D_DN),
                         lambda b, i: (b, jnp.maximum(bidx(i) * sub_per_blk - 1, 0), 0)),
            pl.BlockSpec((1, nb, 3 * D_DN), lambda b, i: (b, bidx(i), 0)),
            pl.BlockSpec((1, SUBLANES, 3 * D_DN),
                         lambda b, i: (b, jnp.minimum((bidx(i) + 1) * sub_per_blk, last_sub), 0)),
            pl.BlockSpec((1, nb, LANES), lambda b, i: (b, bidx(i), 0)),
            pl.BlockSpec((SUBLANES, 3 * D_DN), lambda b, i: (0, 0)),
            state_spec,
        ],
        out_specs=[
            pl.BlockSpec((1, nb, D_DN), lambda b, i: (b, bidx(i), 0)),
            state_spec,
        ],
        out_shape=[
            jax.ShapeDtypeStruct((bsz, t, D_DN), F32),
            jax.ShapeDtypeStruct((bsz, DN_HEADS, DN_HEAD_DIM, DN_HEAD_DIM), F32),
        ],
        scratch_shapes=[
            pltpu.VMEM((nb + 2 * SUBLANES, 3 * D_DN), F32),
            pltpu.VMEM((DN_HEADS, DN_HEAD_DIM, DN_HEAD_DIM), F32),
            pltpu.VMEM((DN_HEADS, nb, LANES), F32),
            pltpu.VMEM((DN_HEADS, CHUNKS_PER_BLOCK, 2 * CHUNK, LANES), BF16),
            pltpu.VMEM((DN_HEADS, CHUNKS_PER_BLOCK, CHUNK, CHUNK), BF16),
            pltpu.VMEM((DN_HEADS, CHUNKS_PER_BLOCK, DN_HEAD_DIM, CHUNK), BF16),
        ],
        compiler_params=pltpu.CompilerParams(
            dimension_semantics=("arbitrary", "arbitrary"), vmem_limit_bytes=VMEM_LIMIT),
        name="delta_bwd" if reverse else "delta_fwd",
    )(qkv, qkv, qkv, gb, scw, s0)


def _outproj_kernel(*refs, final):
    if final:
        (conv_ref, cgs_ref, of_ref, ob_ref, zs_ref, x_ref, cb_ref, lnw_ref, lnb_ref, dnw_ref,
         gate_ref, wc_ref, wd_ref, fnw_ref, o_ref) = refs
    else:
        (conv_ref, cgs_ref, of_ref, ob_ref, zs_ref, x_ref, cb_ref, lnw_ref, lnb_ref, dnw_ref,
         gate_ref, wc_ref, wd_ref, o_ref) = refs
    yc = conv_ref[0] + cb_ref[...]
    mu = jnp.mean(yc, axis=-1, keepdims=True)
    d = yc - mu
    var = jnp.mean(d * d, axis=-1, keepdims=True)
    yn = d * lax.rsqrt(var + LN_EPS) * lnw_ref[...] + lnb_ref[...]
    y_conv = (_silu(yn) * cgs_ref[0]).astype(BF16)
    o = of_ref[0] + ob_ref[0]
    acc = _dot(y_conv, wc_ref[...])
    for h in range(DN_HEADS):
        sl = slice(h * LANES, (h + 1) * LANES)
        oh = o[:, sl]
        on = oh * lax.rsqrt(jnp.mean(oh * oh, axis=-1, keepdims=True) + EPS) * dnw_ref[...]
        y_dn = (on * zs_ref[0, :, sl]).astype(BF16)
        acc = acc + _dot(y_dn, wd_ref[sl, :])
    xn = x_ref[0] + gate_ref[0] * acc
    if final:
        xn = xn * lax.rsqrt(jnp.mean(xn * xn, axis=-1, keepdims=True) + EPS) * fnw_ref[...]
    o_ref[0] = xn


def _outproj(conv, cgs, o_f, o_b, zs, x, cb, lnw, lnb, dnw, gate, wc, wd, fnw, *, tm):
    bsz, t, _ = x.shape
    final = fnw is not None
    row = lambda n: pl.BlockSpec((1, tm, n), lambda b, i: (b, i, 0))
    const = lambda shp: pl.BlockSpec(shp, lambda b, i: (0,) * len(shp))
    per_b = pl.BlockSpec((1, 1, D_MODEL), lambda b, i: (b, 0, 0))
    in_specs = [row(D_CONV), row(D_CONV), row(D_DN), row(D_DN), row(D_DN), row(D_MODEL),
                const((1, D_CONV)), const((1, D_CONV)), const((1, D_CONV)), const((1, DN_HEAD_DIM)),
                per_b, const((D_CONV, D_MODEL)), const((D_DN, D_MODEL))]
    args = [conv, cgs, o_f, o_b, zs, x, cb, lnw, lnb, dnw, gate, wc, wd]
    if final:
        in_specs.append(const((1, D_MODEL)))
        args.append(fnw)
    return pl.pallas_call(
        functools.partial(_outproj_kernel, final=final),
        grid=(bsz, t // tm),
        in_specs=in_specs,
        out_specs=row(D_MODEL),
        out_shape=jax.ShapeDtypeStruct((bsz, t, D_MODEL), F32),
        compiler_params=pltpu.CompilerParams(
            dimension_semantics=("parallel", "parallel"), vmem_limit_bytes=VMEM_LIMIT),
        name="outproj_final" if final else "outproj",
    )(*args)


def _layer_weights(l, norm_w, w_in, conv_w, conv_b, conv_ln_w, conv_ln_b, short_conv_w, a_log,
                   dt_bias, dn_norm_w, w_out):
    wi = w_in[l]
    c0 = 3 * D_CONV
    w_ab = jnp.zeros((D_MODEL, LANES), F32).at[:, :4 * DN_HEADS].set(wi[:, c0 + 4 * D_DN:])
    pad_vec = lambda v: jnp.zeros((1, LANES), F32).at[0, :2 * DN_HEADS].set(v.reshape(-1))
    return dict(
        nw=norm_w[l].reshape(1, D_MODEL),
        w_full=[wi[:, :D_CONV].astype(BF16), wi[:, D_CONV:2 * D_CONV].astype(BF16),
                wi[:, 2 * D_CONV:c0].astype(BF16), wi[:, c0 + 3 * D_DN:c0 + 4 * D_DN].astype(BF16)],
        w_dn=[wi[:, c0:c0 + 3 * D_DN].astype(BF16), w_ab.astype(BF16)],
        alog=pad_vec(a_log[l]), dtb=pad_vec(dt_bias[l]),
        conv_w=jnp.zeros((4 * SUBLANES, D_CONV), F32).at[:CONV_W].set(conv_w[l]),
        conv_b=conv_b[l].reshape(1, D_CONV), lnw=conv_ln_w[l].reshape(1, D_CONV),
        lnb=conv_ln_b[l].reshape(1, D_CONV),
        scw=jnp.zeros((SUBLANES, 3 * D_DN), F32).at[:SHORT_CONV_W].set(short_conv_w[l]),
        dnw=dn_norm_w[l].reshape(1, DN_HEAD_DIM),
        wc=w_out[l, :D_CONV].astype(BF16), wd=w_out[l, D_CONV:].astype(BF16),
    )


def _mixer(x, mod, p, s0_f, s0_b, *, grid_mode, tm, fnw=None, states_only=False):
    bsz = x.shape[0]
    shift, scale, gate = (mod[:, j * D_MODEL:(j + 1) * D_MODEL].reshape(bsz, 1, D_MODEL) for j in range(3))
    if states_only:
        qkv, gb = _inproj(x, p["nw"], shift, scale, p["w_dn"], p["alog"], p["dtb"], full=False, tm=tm)
        _, s_f = _delta(qkv, gb, p["scw"], s0_f, reverse=False)
        _, s_b = _delta(qkv, gb, p["scw"], s0_b, reverse=True)
        return x, s_f, s_b
    u, cgs, zs, qkv, gb = _inproj(x, p["nw"], shift, scale, p["w_full"] + p["w_dn"], p["alog"], p["dtb"],
                                  full=True, tm=tm)
    conv = _dwconv(u, p["conv_w"], grid_mode=grid_mode)
    o_f, s_f = _delta(qkv, gb, p["scw"], s0_f, reverse=False)
    o_b, s_b = _delta(qkv, gb, p["scw"], s0_b, reverse=True)
    x_new = _outproj(conv, cgs, o_f, o_b, zs, x, p["conv_b"], p["lnw"], p["lnb"], p["dnw"], gate,
                     p["wc"], p["wd"], fnw, tm=tm)
    return x_new, s_f, s_b


def kernel(x, c, ctx, c_ctx, norm_w, w_ada, b_ada, w_in, conv_w, conv_b, conv_ln_w, conv_ln_b, short_conv_w, a_log, dt_bias, dn_norm_w, w_out, final_norm_w):
    bsz = x.shape[0]
    cond_rows = jnp.zeros((SUBLANES, D_MODEL), F32).at[:bsz].set(c).at[bsz].set(c_ctx)
    mods = _ada_all(cond_rows, w_ada, b_ada)
    zeros = jnp.zeros((bsz, DN_HEADS, DN_HEAD_DIM, DN_HEAD_DIM), F32)
    for l in range(DEPTH):
        p = _layer_weights(l, norm_w, w_in, conv_w, conv_b, conv_ln_w, conv_ln_b, short_conv_w,
                           a_log, dt_bias, dn_norm_w, w_out)
        mod_ctx = jnp.broadcast_to(mods[l, bsz:bsz + 1], (bsz, 3 * D_MODEL))
        last = l == DEPTH - 1
        ctx_new, s_f, s_b = _mixer(ctx, mod_ctx, p, zeros, zeros, grid_mode=False, tm=ctx.shape[1],
                                   states_only=last)
        x, _, _ = _mixer(x, mods[l, :bsz], p, s_f, s_b, grid_mode=True, tm=512,
                         fnw=final_norm_w.reshape(1, D_MODEL) if last else None)
        ctx = ctx_new
    return x
```

```python
import functools

import jax
import jax.numpy as jnp
from jax import lax
from jax.experimental import pallas as pl
from jax.experimental.pallas import tpu as pltpu

D_MODEL = 1024
DEPTH = 4
GRID_W = 64
D_CONV = 512
CONV_W = 31
CONV_PAD = (CONV_W - 1) // 2
DN_HEADS = 4
DN_HEAD_DIM = 128
D_DN = DN_HEADS * DN_HEAD_DIM
SHORT_CONV_W = 5
SHORT_PAD = (SHORT_CONV_W - 1) // 2
CHUNK = 64
EPS = 1e-6
LN_EPS = 1e-5

LANES = 128
SUBLANES = 8
DELTA_BLOCK = 256
CHUNKS_PER_BLOCK = DELTA_BLOCK // CHUNK
INV_BASE = 16
SEG_GAP = 16
VMEM_LIMIT = 48 * 1024 * 1024

F32 = jnp.float32
BF16 = jnp.bfloat16


def _dot(a, b):
    return jnp.dot(a, b, preferred_element_type=F32)


def _dot_nt(a, b):
    return lax.dot_general(a, b, (((1,), (1,)), ((), ())), preferred_element_type=F32)


def _sigmoid(x):
    return 1.0 / (1.0 + jnp.exp(-x))


def _silu(x):
    return x * _sigmoid(x)


def _softplus(x):
    return jnp.maximum(x, 0.0) + jnp.log1p(jnp.exp(-jnp.abs(x)))


def _dot_mask_f32(mask_bf16, x):
    hi = x.astype(BF16)
    r1 = x - hi.astype(F32)
    mid = r1.astype(BF16)
    lo = (r1 - mid.astype(F32)).astype(BF16)
    return _dot(mask_bf16, hi) + _dot(mask_bf16, mid) + _dot(mask_bf16, lo)


def _ada_kernel(cond_ref, w_ref, b_ref, o_ref):
    s = _silu(cond_ref[...])
    o_ref[0] = jnp.dot(s, w_ref[0], precision=lax.Precision.HIGHEST,
                       preferred_element_type=F32) + b_ref[0]


def _ada_all(cond_rows, w_ada, b_ada):
    tn = 1024
    return pl.pallas_call(
        _ada_kernel,
        grid=(DEPTH, 3 * D_MODEL // tn),
        in_specs=[
            pl.BlockSpec((SUBLANES, D_MODEL), lambda l, j: (0, 0)),
            pl.BlockSpec((1, D_MODEL, tn), lambda l, j: (l, 0, j)),
            pl.BlockSpec((1, 1, tn), lambda l, j: (l, 0, j)),
        ],
        out_specs=pl.BlockSpec((1, SUBLANES, tn), lambda l, j: (l, 0, j)),
        out_shape=jax.ShapeDtypeStruct((DEPTH, SUBLANES, 3 * D_MODEL), F32),
        compiler_params=pltpu.CompilerParams(vmem_limit_bytes=VMEM_LIMIT),
        name="ada",
    )(cond_rows, w_ada, b_ada.reshape(DEPTH, 1, 3 * D_MODEL))


def _inproj_kernel(*refs, full, tm, nblk):
    if full:
        (prev_ref, x_ref, next_ref, nw_ref, shift_ref, scale_ref, wa_ref, wb_ref, wcg_ref, wz_ref,
         wqkv_ref, wab_ref, alog_ref, dtb_ref, scw_ref,
         u_ref, cgs_ref, zs_ref, q_ref, k_ref, v_ref, gb_ref, qkv_scr) = refs
    else:
        (prev_ref, x_ref, next_ref, nw_ref, shift_ref, scale_ref, wqkv_ref, wab_ref, alog_ref, dtb_ref,
         scw_ref, q_ref, k_ref, v_ref, gb_ref, qkv_scr) = refs
    i = pl.program_id(1)
    x = jnp.concatenate([prev_ref[0], x_ref[0], next_ref[0]], axis=0)
    y = x * lax.rsqrt(jnp.mean(x * x, axis=-1, keepdims=True) + EPS) * nw_ref[...]
    hf = y * (1.0 + scale_ref[0]) + shift_ref[0]
    h = hf[SUBLANES:SUBLANES + tm].astype(BF16)
    if full:
        u_ref[0] = _dot(h, wa_ref[...]) * _sigmoid(_dot(h, wb_ref[...]))
        cgs_ref[0] = _silu(_dot(h, wcg_ref[...]))
        zs_ref[0] = _silu(_dot(h, wz_ref[...]))
    ab = _dot(h, wab_ref[...])
    g = -jnp.exp(alog_ref[...]) * _softplus(ab + dtb_ref[...])
    lane = lax.broadcasted_iota(jnp.int32, ab.shape, 1)
    gb_ref[0] = jnp.where(lane < 2 * DN_HEADS, g, _sigmoid(ab))

    rows = lax.broadcasted_iota(jnp.int32, (tm + 2 * SUBLANES, 1), 0)
    first_row = jnp.where(i > 0, 0, SUBLANES)
    end_row = jnp.where(i < nblk - 1, tm + 2 * SUBLANES, tm + SUBLANES)
    inside = jnp.logical_and(rows >= first_row, rows < end_row)
    qkv = jnp.where(inside, _dot(hf.astype(BF16), wqkv_ref[...]), 0.0)
    for grp in range(3 * DN_HEADS):
        qkv_scr[grp] = qkv[:, grp * LANES:(grp + 1) * LANES]

    def conv_silu(grp):
        acc = jnp.zeros((tm, LANES), F32)
        for j in range(SHORT_CONV_W):
            acc = acc + (qkv_scr[grp, pl.ds(SUBLANES - SHORT_PAD + j, tm), :]
                         * scw_ref[j:j + 1, grp * LANES:(grp + 1) * LANES])
        return _silu(acc)

    def l2n(m):
        return m * lax.rsqrt(jnp.sum(m * m, axis=-1, keepdims=True) + EPS)

    for hd in range(DN_HEADS):
        sl = slice(hd * LANES, (hd + 1) * LANES)
        q_ref[0, :, sl] = l2n(conv_silu(hd)) * (DN_HEAD_DIM ** -0.5)
        k_ref[0, :, sl] = l2n(conv_silu(DN_HEADS + hd))
        v_ref[0, :, sl] = conv_silu(2 * DN_HEADS + hd)


def _inproj(x, nw, shift, scale, wts, alog, dtb, scw, *, full, tm):
    bsz, t, _ = x.shape
    nblk = t // tm
    sub_per_blk = tm // SUBLANES
    last_sub = t // SUBLANES - 1
    row = lambda n: pl.BlockSpec((1, tm, n), lambda b, i: (b, i, 0))
    const = lambda shp: pl.BlockSpec(shp, lambda b, i: (0,) * len(shp))
    per_b = pl.BlockSpec((1, 1, D_MODEL), lambda b, i: (b, 0, 0))
    halo_prev = pl.BlockSpec((1, SUBLANES, D_MODEL),
                             lambda b, i: (b, jnp.maximum(i * sub_per_blk - 1, 0), 0))
    halo_next = pl.BlockSpec((1, SUBLANES, D_MODEL),
                             lambda b, i: (b, jnp.minimum((i + 1) * sub_per_blk, last_sub), 0))
    w_specs = [const(w.shape) for w in wts]
    in_specs = ([halo_prev, row(D_MODEL), halo_next, const((1, D_MODEL)), per_b, per_b] + w_specs
                + [const((1, LANES))] * 2 + [const((SUBLANES, 3 * D_DN))])
    widths = ([D_CONV, D_CONV, D_DN] if full else []) + [D_DN, D_DN, D_DN, LANES]
    return pl.pallas_call(
        functools.partial(_inproj_kernel, full=full, tm=tm, nblk=nblk),
        grid=(bsz, nblk),
        in_specs=in_specs,
        out_specs=[row(n) for n in widths],
        out_shape=[jax.ShapeDtypeStruct((bsz, t, n), F32) for n in widths],
        scratch_shapes=[pltpu.VMEM((3 * DN_HEADS, tm + 2 * SUBLANES, LANES), F32)],
        compiler_params=pltpu.CompilerParams(
            dimension_semantics=("parallel", "parallel"), vmem_limit_bytes=VMEM_LIMIT),
        name="inproj_full" if full else "inproj_dn",
    )(x, x, x, nw, shift, scale, *wts, alog, dtb, scw)


def _conv_segments(u_ref, w_ref, o_ref, pad_ref, *, seg, nseg):
    stride = seg + SEG_GAP
    zeros_gap = jnp.zeros((SEG_GAP, LANES), F32)

    def fill(r, carry):
        base = pl.multiple_of(r * stride, SUBLANES)
        src = pl.multiple_of(r * seg, SUBLANES)
        pad_ref[pl.ds(base, SEG_GAP), :] = zeros_gap
        pad_ref[pl.ds(base + SEG_GAP, seg), :] = u_ref[0, pl.ds(src, seg), :]
        return carry

    lax.fori_loop(0, nseg, fill, 0)
    pad_ref[pl.ds(nseg * stride, SEG_GAP), :] = zeros_gap

    def body(r, carry):
        base = pl.multiple_of(r * stride, SUBLANES)
        dst = pl.multiple_of(r * seg, SUBLANES)
        acc = jnp.zeros((seg, LANES), F32)
        for k in range(CONV_W):
            acc = acc + pad_ref[pl.ds(base + (SEG_GAP - CONV_PAD) + k, seg), :] * w_ref[k:k + 1, :]
        o_ref[0, pl.ds(dst, seg), :] = acc
        return carry

    lax.fori_loop(0, nseg, body, 0)


def _conv_strided(u_ref, w_ref, o_ref, pad_ref, *, t, step):
    halo = CONV_PAD * step
    pad_ref[pl.ds(0, halo), :] = jnp.zeros((halo, LANES), F32)
    pad_ref[pl.ds(halo + t, halo), :] = jnp.zeros((halo, LANES), F32)
    pad_ref[pl.ds(halo, t), :] = u_ref[0]

    def body(r, carry):
        base = pl.multiple_of(r * step, SUBLANES)
        acc = jnp.zeros((step, LANES), F32)
        for k in range(CONV_W):
            acc = acc + pad_ref[pl.ds(base + k * step, step), :] * w_ref[k:k + 1, :]
        o_ref[0, pl.ds(base, step), :] = acc
        return carry

    lax.fori_loop(0, t // step, body, 0)


def _grid_conv_kernel(u_ref, w_ref, o_ref, pad_ref, *, t):
    j = pl.program_id(1)
    half_blocks = D_CONV // LANES // 2

    @pl.when(j < half_blocks)
    def _():
        _conv_segments(u_ref, w_ref, o_ref, pad_ref, seg=GRID_W, nseg=t // GRID_W)

    @pl.when(j >= half_blocks)
    def _():
        _conv_strided(u_ref, w_ref, o_ref, pad_ref, t=t, step=GRID_W)


def _seq_conv_kernel(u_ref, w_ref, o_ref, pad_ref, *, t):
    _conv_segments(u_ref, w_ref, o_ref, pad_ref, seg=t, nseg=1)


def _dwconv(u, w_pad, *, grid_mode):
    bsz, t, _ = u.shape
    if grid_mode:
        rows = t // GRID_W
        pad_rows = max(rows * (GRID_W + SEG_GAP) + SEG_GAP, t + 2 * CONV_PAD * GRID_W)
        body = functools.partial(_grid_conv_kernel, t=t)
    else:
        pad_rows = t + 2 * SEG_GAP
        body = functools.partial(_seq_conv_kernel, t=t)
    return pl.pallas_call(
        body,
        grid=(bsz, D_CONV // LANES),
        in_specs=[
            pl.BlockSpec((1, t, LANES), lambda b, j: (b, 0, j)),
            pl.BlockSpec((4 * SUBLANES, LANES), lambda b, j: (0, j)),
        ],
        out_specs=pl.BlockSpec((1, t, LANES), lambda b, j: (b, 0, j)),
        out_shape=jax.ShapeDtypeStruct((bsz, t, D_CONV), F32),
        scratch_shapes=[pltpu.VMEM((pad_rows, LANES), F32)],
        compiler_params=pltpu.CompilerParams(
            dimension_semantics=("parallel", "parallel"), vmem_limit_bytes=VMEM_LIMIT),
        name="grid_conv" if grid_mode else "seq_conv",
    )(u, w_pad)


def _delta_kernel(q_ref, k_ref, v_ref, gb_ref, s0_ref, o_ref, sfin_ref,
                  s_scr, u_scr, wq_scr, attn_scr, kt_scr, *, reverse, nblk):
    nb = DELTA_BLOCK
    i = pl.program_id(1)

    @pl.when(i == 0)
    def _():
        s_scr[...] = s0_ref[0]

    row = lax.broadcasted_iota(jnp.int32, (nb, nb), 0)
    col = lax.broadcasted_iota(jnp.int32, (nb, nb), 1)
    same = (row // CHUNK) == (col // CHUNK)
    if reverse:
        incl = jnp.logical_and(same, col >= row)
        strict = jnp.logical_and(same, col > row)
    else:
        incl = jnp.logical_and(same, col <= row)
        strict = jnp.logical_and(same, col < row)
    base_blocks = (row // INV_BASE) == (col // INV_BASE)
    merge_masks = []
    size = INV_BASE
    while size < CHUNK:
        merge_masks.append(jnp.logical_and((row // (2 * size)) == (col // (2 * size)),
                                           (row // size) != (col // size)))
        size *= 2

    gbv = gb_ref[0]
    gc = _dot_mask_f32(jnp.where(incl, 1.0, 0.0).astype(BF16), gbv)
    tot = _dot_mask_f32(jnp.where(same, 1.0, 0.0).astype(BF16), gbv)
    gc_t = gc.T

    heads = range(DN_HEADS)
    lgs = [(DN_HEADS if reverse else 0) + h for h in heads]
    q = [q_ref[0, :, h * LANES:(h + 1) * LANES] for h in heads]
    k = [k_ref[0, :, h * LANES:(h + 1) * LANES] for h in heads]
    v = [v_ref[0, :, h * LANES:(h + 1) * LANES] for h in heads]
    gcol = [gc[:, lg:lg + 1] for lg in lgs]
    tcol = [tot[:, lg:lg + 1] for lg in lgs]
    beta = [gbv[:, 2 * DN_HEADS + lg:2 * DN_HEADS + lg + 1] for lg in lgs]
    decay = [jnp.where(incl, jnp.exp(jnp.where(incl, gcol[h] - gc_t[lgs[h]:lgs[h] + 1, :], 0.0)), 0.0)
             for h in heads]
    kb = [k[h] * beta[h] for h in heads]
    kbf = [k[h].astype(BF16) for h in heads]
    eg = [jnp.exp(gcol[h]) for h in heads]
    lfull = [_dot_nt(kb[h].astype(BF16), kbf[h]) * jnp.where(strict, decay[h], 0.0) for h in heads]
    l0 = [jnp.where(base_blocks, lfull[h], 0.0) for h in heads]
    n = [-l0[h] for h in heads]
    pb = [l0[h].astype(BF16) for h in heads]
    p = [_dot(pb[h], pb[h]) for h in heads]
    for it in range(3):
        pb = [p[h].astype(BF16) for h in heads]
        n = [_dot(n[h].astype(BF16), pb[h]) + (n[h] + p[h]) for h in heads]
        if it < 2:
            p = [_dot(pb[h], pb[h]) for h in heads]
    for pair_mask in merge_masks:
        cm = [jnp.where(pair_mask, lfull[h], 0.0) for h in heads]
        nbf = [n[h].astype(BF16) for h in heads]
        xm = [_dot(cm[h].astype(BF16), nbf[h]) + cm[h] for h in heads]
        n = [n[h] - (_dot(nbf[h], xm[h].astype(BF16)) + xm[h]) for h in heads]
    rhs = [jnp.concatenate([v[h] * beta[h], kb[h] * eg[h]], axis=-1) for h in heads]
    y = [_dot(n[h].astype(BF16), rhs[h].astype(BF16)) + rhs[h] for h in heads]
    attn = [_dot_nt(q[h].astype(BF16), kbf[h]) * decay[h] for h in heads]
    gl = []
    for h in heads:
        kt_t = (k[h] * jnp.exp(tcol[h] - gcol[h])).T
        qg = q[h] * eg[h]
        u_scr[h] = y[h][:, :LANES]
        for c in range(CHUNKS_PER_BLOCK):
            sl = slice(c * CHUNK, (c + 1) * CHUNK)
            wq_scr[h, c, 0:CHUNK, :] = y[h][sl, LANES:].astype(BF16)
            wq_scr[h, c, CHUNK:2 * CHUNK, :] = qg[sl, :].astype(BF16)
            attn_scr[h, c] = attn[h][sl, sl].astype(BF16)
            kt_scr[h, c] = kt_t[:, sl].astype(BF16)
        gl.append([jnp.exp(tot[c * CHUNK:c * CHUNK + 1, lgs[h]:lgs[h] + 1])
                   for c in range(CHUNKS_PER_BLOCK)])

    for step in range(CHUNKS_PER_BLOCK):
        c = (CHUNKS_PER_BLOCK - 1 - step) if reverse else step
        for h in range(DN_HEADS):
            s = s_scr[h]
            r = _dot(wq_scr[h, c], s.astype(BF16))
            v_new = (u_scr[h, pl.ds(c * CHUNK, CHUNK), :] - r[:CHUNK]).astype(BF16)
            o_ref[0, pl.ds(c * CHUNK, CHUNK), h * LANES:(h + 1) * LANES] = (
                _dot(attn_scr[h, c], v_new) + r[CHUNK:])
            s_scr[h] = _dot(kt_scr[h, c], v_new) + s * gl[h][c]

    @pl.when(i == nblk - 1)
    def _():
        sfin_ref[0] = s_scr[...]


def _delta(q, k, v, gb, s0, *, reverse):
    bsz, t, _ = q.shape
    nb = DELTA_BLOCK
    nblk = t // nb
    if reverse:
        bidx = lambda i: nblk - 1 - i
    else:
        bidx = lambda i: i
    blk = lambda n: pl.BlockSpec((1, nb, n), lambda b, i: (b, bidx(i), 0))
    state_spec = pl.BlockSpec((1, DN_HEADS, DN_HEAD_DIM, DN_HEAD_DIM), lambda b, i: (b, 0, 0, 0))
    return pl.pallas_call(
        functools.partial(_delta_kernel, reverse=reverse, nblk=nblk),
        grid=(bsz, nblk),
        in_specs=[blk(D_DN), blk(D_DN), blk(D_DN), blk(LANES), state_spec],
        out_specs=[blk(D_DN), state_spec],
        out_shape=[
            jax.ShapeDtypeStruct((bsz, t, D_DN), F32),
            jax.ShapeDtypeStruct((bsz, DN_HEADS, DN_HEAD_DIM, DN_HEAD_DIM), F32),
        ],
        scratch_shapes=[
            pltpu.VMEM((DN_HEADS, DN_HEAD_DIM, DN_HEAD_DIM), F32),
            pltpu.VMEM((DN_HEADS, nb, LANES), F32),
            pltpu.VMEM((DN_HEADS, CHUNKS_PER_BLOCK, 2 * CHUNK, LANES), BF16),
            pltpu.VMEM((DN_HEADS, CHUNKS_PER_BLOCK, CHUNK, CHUNK), BF16),
            pltpu.VMEM((DN_HEADS, CHUNKS_PER_BLOCK, DN_HEAD_DIM, CHUNK), BF16),
        ],
        compiler_params=pltpu.CompilerParams(
            dimension_semantics=("arbitrary", "arbitrary"), vmem_limit_bytes=VMEM_LIMIT),
        name="delta_bwd" if reverse else "delta_fwd",
    )(q, k, v, gb, s0)


def _outproj_kernel(*refs, final):
    if final:
        (conv_ref, cgs_ref, of_ref, ob_ref, zs_ref, x_ref, cb_ref, lnw_ref, lnb_ref, dnw_ref,
         gate_ref, wc_ref, wd_ref, fnw_ref, o_ref) = refs
    else:
        (conv_ref, cgs_ref, of_ref, ob_ref, zs_ref, x_ref, cb_ref, lnw_ref, lnb_ref, dnw_ref,
         gate_ref, wc_ref, wd_ref, o_ref) = refs
    yc = conv_ref[0] + cb_ref[...]
    mu = jnp.mean(yc, axis=-1, keepdims=True)
    d = yc - mu
    var = jnp.mean(d * d, axis=-1, keepdims=True)
    yn = d * lax.rsqrt(var + LN_EPS) * lnw_ref[...] + lnb_ref[...]
    y_conv = (_silu(yn) * cgs_ref[0]).astype(BF16)
    o = of_ref[0] + ob_ref[0]
    acc = _dot(y_conv, wc_ref[...])
    for h in range(DN_HEADS):
        sl = slice(h * LANES, (h + 1) * LANES)
        oh = o[:, sl]
        on = oh * lax.rsqrt(jnp.mean(oh * oh, axis=-1, keepdims=True) + EPS) * dnw_ref[...]
        y_dn = (on * zs_ref[0, :, sl]).astype(BF16)
        acc = acc + _dot(y_dn, wd_ref[sl, :])
    xn = x_ref[0] + gate_ref[0] * acc
    if final:
        xn = xn * lax.rsqrt(jnp.mean(xn * xn, axis=-1, keepdims=True) + EPS) * fnw_ref[...]
    o_ref[0] = xn


def _outproj(conv, cgs, o_f, o_b, zs, x, cb, lnw, lnb, dnw, gate, wc, wd, fnw, *, tm):
    bsz, t, _ = x.shape
    final = fnw is not None
    row = lambda n: pl.BlockSpec((1, tm, n), lambda b, i: (b, i, 0))
    const = lambda shp: pl.BlockSpec(shp, lambda b, i: (0,) * len(shp))
    per_b = pl.BlockSpec((1, 1, D_MODEL), lambda b, i: (b, 0, 0))
    in_specs = [row(D_CONV), row(D_CONV), row(D_DN), row(D_DN), row(D_DN), row(D_MODEL),
                const((1, D_CONV)), const((1, D_CONV)), const((1, D_CONV)), const((1, DN_HEAD_DIM)),
                per_b, const((D_CONV, D_MODEL)), const((D_DN, D_MODEL))]
    args = [conv, cgs, o_f, o_b, zs, x, cb, lnw, lnb, dnw, gate, wc, wd]
    if final:
        in_specs.append(const((1, D_MODEL)))
        args.append(fnw)
    return pl.pallas_call(
        functools.partial(_outproj_kernel, final=final),
        grid=(bsz, t // tm),
        in_specs=in_specs,
        out_specs=row(D_MODEL),
        out_shape=jax.ShapeDtypeStruct((bsz, t, D_MODEL), F32),
        compiler_params=pltpu.CompilerParams(
            dimension_semantics=("parallel", "parallel"), vmem_limit_bytes=VMEM_LIMIT),
        name="outproj_final" if final else "outproj",
    )(*args)


def _layer_weights(l, norm_w, w_in, conv_w, conv_b, conv_ln_w, conv_ln_b, short_conv_w, a_log,
                   dt_bias, dn_norm_w, w_out):
    wi = w_in[l]
    c0 = 3 * D_CONV
    w_ab = jnp.zeros((D_MODEL, LANES), F32).at[:, :4 * DN_HEADS].set(wi[:, c0 + 4 * D_DN:])
    pad_vec = lambda v: jnp.zeros((1, LANES), F32).at[0, :2 * DN_HEADS].set(v.reshape(-1))
    return dict(
        nw=norm_w[l].reshape(1, D_MODEL),
        w_full=[wi[:, :D_CONV].astype(BF16), wi[:, D_CONV:2 * D_CONV].astype(BF16),
                wi[:, 2 * D_CONV:c0].astype(BF16), wi[:, c0 + 3 * D_DN:c0 + 4 * D_DN].astype(BF16)],
        w_dn=[wi[:, c0:c0 + 3 * D_DN].astype(BF16), w_ab.astype(BF16)],
        alog=pad_vec(a_log[l]), dtb=pad_vec(dt_bias[l]),
        conv_w=jnp.zeros((4 * SUBLANES, D_CONV), F32).at[:CONV_W].set(conv_w[l]),
        conv_b=conv_b[l].reshape(1, D_CONV), lnw=conv_ln_w[l].reshape(1, D_CONV),
        lnb=conv_ln_b[l].reshape(1, D_CONV),
        scw=jnp.zeros((SUBLANES, 3 * D_DN), F32).at[:SHORT_CONV_W].set(short_conv_w[l]),
        dnw=dn_norm_w[l].reshape(1, DN_HEAD_DIM),
        wc=w_out[l, :D_CONV].astype(BF16), wd=w_out[l, D_CONV:].astype(BF16),
    )


def _mixer(x, mod, p, s0_f, s0_b, *, grid_mode, tm, fnw=None, states_only=False):
    bsz = x.shape[0]
    shift, scale, gate = (mod[:, j * D_MODEL:(j + 1) * D_MODEL].reshape(bsz, 1, D_MODEL) for j in range(3))
    if states_only:
        q, k, v, gb = _inproj(x, p["nw"], shift, scale, p["w_dn"], p["alog"], p["dtb"], p["scw"],
                              full=False, tm=tm)
        _, s_f = _delta(q, k, v, gb, s0_f, reverse=False)
        _, s_b = _delta(q, k, v, gb, s0_b, reverse=True)
        return x, s_f, s_b
    u, cgs, zs, q, k, v, gb = _inproj(x, p["nw"], shift, scale, p["w_full"] + p["w_dn"], p["alog"],
                                      p["dtb"], p["scw"], full=True, tm=tm)
    conv = _dwconv(u, p["conv_w"], grid_mode=grid_mode)
    o_f, s_f = _delta(q, k, v, gb, s0_f, reverse=False)
    o_b, s_b = _delta(q, k, v, gb, s0_b, reverse=True)
    x_new = _outproj(conv, cgs, o_f, o_b, zs, x, p["conv_b"], p["lnw"], p["lnb"], p["dnw"], gate,
                     p["wc"], p["wd"], fnw, tm=tm)
    return x_new, s_f, s_b


def kernel(x, c, ctx, c_ctx, norm_w, w_ada, b_ada, w_in, conv_w, conv_b, conv_ln_w, conv_ln_b, short_conv_w, a_log, dt_bias, dn_norm_w, w_out, final_norm_w):
    bsz = x.shape[0]
    cond_rows = jnp.zeros((SUBLANES, D_MODEL), F32).at[:bsz].set(c).at[bsz].set(c_ctx)
    mods = _ada_all(cond_rows, w_ada, b_ada)
    zeros = jnp.zeros((bsz, DN_HEADS, DN_HEAD_DIM, DN_HEAD_DIM), F32)
    for l in range(DEPTH):
        p = _layer_weights(l, norm_w, w_in, conv_w, conv_b, conv_ln_w, conv_ln_b, short_conv_w,
                           a_log, dt_bias, dn_norm_w, w_out)
        mod_ctx = jnp.broadcast_to(mods[l, bsz:bsz + 1], (bsz, 3 * D_MODEL))
        last = l == DEPTH - 1
        ctx_new, s_f, s_b = _mixer(ctx, mod_ctx, p, zeros, zeros, grid_mode=False, tm=ctx.shape[1],
                                   states_only=last)
        x, _, _ = _mixer(x, mods[l, :bsz], p, s_f, s_b, grid_mode=True, tm=512,
                         fnw=final_norm_w.reshape(1, D_MODEL) if last else None)
        ctx = ctx_new
    return x
```

```python
import functools

import jax
import jax.numpy as jnp
from jax import lax
from jax.experimental import pallas as pl
from jax.experimental.pallas import tpu as pltpu

D_MODEL = 1024
DEPTH = 4
GRID_W = 64
D_CONV = 512
CONV_W = 31
CONV_PAD = (CONV_W - 1) // 2
DN_HEADS = 4
DN_HEAD_DIM = 128
D_DN = DN_HEADS * DN_HEAD_DIM
SHORT_CONV_W = 5
SHORT_PAD = (SHORT_CONV_W - 1) // 2
CHUNK = 64
EPS = 1e-6
LN_EPS = 1e-5

LANES = 128
SUBLANES = 8
DELTA_BLOCK = 256
CHUNKS_PER_BLOCK = DELTA_BLOCK // CHUNK
INV_BASE = 16
BF16_ROWS = 16
HALO = BF16_ROWS
SEG_GAP = 16
VMEM_LIMIT = 48 * 1024 * 1024

F32 = jnp.float32
BF16 = jnp.bfloat16


def _dot(a, b):
    return jnp.dot(a, b, preferred_element_type=F32)


def _dot_nt(a, b):
    return lax.dot_general(a, b, (((1,), (1,)), ((), ())), preferred_element_type=F32)


def _sigmoid(x):
    return 1.0 / (1.0 + jnp.exp(-x))


def _silu(x):
    return x * _sigmoid(x)


def _softplus(x):
    return jnp.maximum(x, 0.0) + jnp.log(1.0 + jnp.exp(-jnp.abs(x)))


def _dot_mask_f32(mask_bf16, x):
    hi = x.astype(BF16)
    r1 = x - hi.astype(F32)
    mid = r1.astype(BF16)
    lo = (r1 - mid.astype(F32)).astype(BF16)
    return _dot(mask_bf16, hi) + _dot(mask_bf16, mid) + _dot(mask_bf16, lo)


def _ada_kernel(cond_ref, w_ref, b_ref, o_ref):
    s = _silu(cond_ref[...])
    o_ref[0] = jnp.dot(s, w_ref[0], precision=lax.Precision.HIGHEST,
                       preferred_element_type=F32) + b_ref[0]


def _ada_all(cond_rows, w_ada, b_ada):
    tn = 1024
    return pl.pallas_call(
        _ada_kernel,
        grid=(DEPTH, 3 * D_MODEL // tn),
        in_specs=[
            pl.BlockSpec((SUBLANES, D_MODEL), lambda l, j: (0, 0)),
            pl.BlockSpec((1, D_MODEL, tn), lambda l, j: (l, 0, j)),
            pl.BlockSpec((1, 1, tn), lambda l, j: (l, 0, j)),
        ],
        out_specs=pl.BlockSpec((1, SUBLANES, tn), lambda l, j: (l, 0, j)),
        out_shape=jax.ShapeDtypeStruct((DEPTH, SUBLANES, 3 * D_MODEL), F32),
        compiler_params=pltpu.CompilerParams(vmem_limit_bytes=VMEM_LIMIT),
        name="ada",
    )(cond_rows, w_ada, b_ada.reshape(DEPTH, 1, 3 * D_MODEL))


def _inproj_kernel(*refs, full, tm, nblk):
    if full:
        (prev_ref, x_ref, next_ref, nw_ref, shift_ref, scale_ref, wa_ref, wb_ref, wcg_ref, wz_ref,
         wqkv_ref, wab_ref, alog_ref, dtb_ref, scw_ref,
         u_ref, cgs_ref, zs_ref, q_ref, k_ref, v_ref, gb_ref, qkv_scr) = refs
    else:
        (prev_ref, x_ref, next_ref, nw_ref, shift_ref, scale_ref, wqkv_ref, wab_ref, alog_ref, dtb_ref,
         scw_ref, q_ref, k_ref, v_ref, gb_ref, qkv_scr) = refs
    i = pl.program_id(1)

    def modulated(x):
        y = x * lax.rsqrt(jnp.mean(x * x, axis=-1, keepdims=True) + EPS) * nw_ref[...]
        return (y * (1.0 + scale_ref[0]) + shift_ref[0]).astype(BF16)

    h = modulated(x_ref[0])
    if full:
        u_ref[0] = _dot(h, wa_ref[...]) * _sigmoid(_dot(h, wb_ref[...]))
        cgs_ref[0] = _silu(_dot(h, wcg_ref[...])).astype(cgs_ref.dtype)
        zs_ref[0] = _silu(_dot(h, wz_ref[...])).astype(zs_ref.dtype)
    ab_t = _dot(h, wab_ref[...]).T
    a_t = ab_t[0:2 * DN_HEADS]
    g_t = -jnp.exp(alog_ref[...]) * _softplus(a_t + dtb_ref[...])
    beta_t = _sigmoid(ab_t[2 * DN_HEADS:4 * DN_HEADS])
    rest = jnp.zeros((LANES - 4 * DN_HEADS, tm), F32)
    gb_ref[0] = jnp.concatenate([g_t, beta_t, rest], axis=0).T

    h_ext = jnp.concatenate([modulated(prev_ref[0]), h, modulated(next_ref[0])], axis=0)
    rows = lax.broadcasted_iota(jnp.int32, (tm + 2 * HALO, 1), 0)
    first_row = jnp.where(i > 0, 0, HALO)
    end_row = jnp.where(i < nblk - 1, tm + 2 * HALO, tm + HALO)
    inside = jnp.logical_and(rows >= first_row, rows < end_row)
    qkv = jnp.where(inside, _dot(h_ext, wqkv_ref[...]), 0.0)
    for grp in range(3 * DN_HEADS):
        qkv_scr[grp] = qkv[:, grp * LANES:(grp + 1) * LANES]

    def conv_silu(grp):
        acc = jnp.zeros((tm, LANES), F32)
        for j in range(SHORT_CONV_W):
            acc = acc + (qkv_scr[grp, pl.ds(HALO - SHORT_PAD + j, tm), :]
                         * scw_ref[j:j + 1, grp * LANES:(grp + 1) * LANES])
        return _silu(acc)

    def l2n(m):
        return m * lax.rsqrt(jnp.sum(m * m, axis=-1, keepdims=True) + EPS)

    for hd in range(DN_HEADS):
        sl = slice(hd * LANES, (hd + 1) * LANES)
        q_ref[0, :, sl] = (l2n(conv_silu(hd)) * (DN_HEAD_DIM ** -0.5)).astype(q_ref.dtype)
        k_ref[0, :, sl] = l2n(conv_silu(DN_HEADS + hd)).astype(k_ref.dtype)
        v_ref[0, :, sl] = conv_silu(2 * DN_HEADS + hd).astype(v_ref.dtype)


def _inproj(x, nw, shift, scale, wts, alog, dtb, scw, *, full, tm):
    bsz, t, _ = x.shape
    nblk = t // tm
    halo_per_blk = tm // HALO
    last_halo = t // HALO - 1
    row = lambda n: pl.BlockSpec((1, tm, n), lambda b, i: (b, i, 0))
    const = lambda shp: pl.BlockSpec(shp, lambda b, i: (0,) * len(shp))
    per_b = pl.BlockSpec((1, 1, D_MODEL), lambda b, i: (b, 0, 0))
    halo_prev = pl.BlockSpec((1, HALO, D_MODEL),
                             lambda b, i: (b, jnp.maximum(i * halo_per_blk - 1, 0), 0))
    halo_next = pl.BlockSpec((1, HALO, D_MODEL),
                             lambda b, i: (b, jnp.minimum((i + 1) * halo_per_blk, last_halo), 0))
    w_specs = [const(w.shape) for w in wts]
    in_specs = ([halo_prev, row(D_MODEL), halo_next, const((1, D_MODEL)), per_b, per_b] + w_specs
                + [const((2 * DN_HEADS, 1))] * 2 + [const((SUBLANES, 3 * D_DN))])
    outs = ([(D_CONV, F32), (D_CONV, BF16), (D_DN, BF16)] if full else []) + [
        (D_DN, BF16), (D_DN, BF16), (D_DN, BF16), (LANES, F32)]
    return pl.pallas_call(
        functools.partial(_inproj_kernel, full=full, tm=tm, nblk=nblk),
        grid=(bsz, nblk),
        in_specs=in_specs,
        out_specs=[row(n) for n, _ in outs],
        out_shape=[jax.ShapeDtypeStruct((bsz, t, n), dt) for n, dt in outs],
        scratch_shapes=[pltpu.VMEM((3 * DN_HEADS, tm + 2 * HALO, LANES), F32)],
        compiler_params=pltpu.CompilerParams(
            dimension_semantics=("parallel", "parallel"), vmem_limit_bytes=VMEM_LIMIT),
        name="inproj_full" if full else "inproj_dn",
    )(x, x, x, nw, shift, scale, *wts, alog, dtb, scw)


def _conv_segments(u_ref, w_ref, o_ref, pad_ref, *, seg, nseg):
    stride = seg + SEG_GAP
    zeros_gap = jnp.zeros((SEG_GAP, LANES), F32)

    def fill(r, carry):
        base = pl.multiple_of(r * stride, SUBLANES)
        src = pl.multiple_of(r * seg, SUBLANES)
        pad_ref[pl.ds(base, SEG_GAP), :] = zeros_gap
        pad_ref[pl.ds(base + SEG_GAP, seg), :] = u_ref[0, pl.ds(src, seg), :]
        return carry

    lax.fori_loop(0, nseg, fill, 0)
    pad_ref[pl.ds(nseg * stride, SEG_GAP), :] = zeros_gap

    def body(r, carry):
        base = pl.multiple_of(r * stride, SUBLANES)
        dst = pl.multiple_of(r * seg, BF16_ROWS)
        acc = jnp.zeros((seg, LANES), F32)
        for k in range(CONV_W):
            acc = acc + pad_ref[pl.ds(base + (SEG_GAP - CONV_PAD) + k, seg), :] * w_ref[k:k + 1, :]
        o_ref[0, pl.ds(dst, seg), :] = acc.astype(o_ref.dtype)
        return carry

    lax.fori_loop(0, nseg, body, 0)


def _conv_strided(u_ref, w_ref, o_ref, pad_ref, *, t, step):
    halo = CONV_PAD * step
    pad_ref[pl.ds(0, halo), :] = jnp.zeros((halo, LANES), F32)
    pad_ref[pl.ds(halo + t, halo), :] = jnp.zeros((halo, LANES), F32)
    pad_ref[pl.ds(halo, t), :] = u_ref[0]

    def body(r, carry):
        base = pl.multiple_of(r * step, BF16_ROWS)
        acc = jnp.zeros((step, LANES), F32)
        for k in range(CONV_W):
            acc = acc + pad_ref[pl.ds(base + k * step, step), :] * w_ref[k:k + 1, :]
        o_ref[0, pl.ds(base, step), :] = acc.astype(o_ref.dtype)
        return carry

    lax.fori_loop(0, t // step, body, 0)


def _grid_conv_kernel(u_ref, w_ref, o_ref, pad_ref, *, t):
    j = pl.program_id(1)
    half_blocks = D_CONV // LANES // 2

    @pl.when(j < half_blocks)
    def _():
        _conv_segments(u_ref, w_ref, o_ref, pad_ref, seg=GRID_W, nseg=t // GRID_W)

    @pl.when(j >= half_blocks)
    def _():
        _conv_strided(u_ref, w_ref, o_ref, pad_ref, t=t, step=GRID_W)


def _seq_conv_kernel(u_ref, w_ref, o_ref, pad_ref, *, t):
    _conv_segments(u_ref, w_ref, o_ref, pad_ref, seg=t, nseg=1)


def _dwconv(u, w_pad, *, grid_mode):
    bsz, t, _ = u.shape
    if grid_mode:
        rows = t // GRID_W
        pad_rows = max(rows * (GRID_W + SEG_GAP) + SEG_GAP, t + 2 * CONV_PAD * GRID_W)
        body = functools.partial(_grid_conv_kernel, t=t)
    else:
        pad_rows = t + 2 * SEG_GAP
        body = functools.partial(_seq_conv_kernel, t=t)
    return pl.pallas_call(
        body,
        grid=(bsz, D_CONV // LANES),
        in_specs=[
            pl.BlockSpec((1, t, LANES), lambda b, j: (b, 0, j)),
            pl.BlockSpec((4 * SUBLANES, LANES), lambda b, j: (0, j)),
        ],
        out_specs=pl.BlockSpec((1, t, LANES), lambda b, j: (b, 0, j)),
        out_shape=jax.ShapeDtypeStruct((bsz, t, D_CONV), BF16),
        scratch_shapes=[pltpu.VMEM((pad_rows, LANES), F32)],
        compiler_params=pltpu.CompilerParams(
            dimension_semantics=("parallel", "parallel"), vmem_limit_bytes=VMEM_LIMIT),
        name="grid_conv" if grid_mode else "seq_conv",
    )(u, w_pad)


def _delta_kernel(q_ref, k_ref, v_ref, gb_ref, s0_ref, o_ref, sfin_ref,
                  s_scr, u_scr, wq_scr, attn_scr, kt_scr, gl_scr, *, reverse, nblk):
    nb = DELTA_BLOCK
    i = pl.program_id(1)
    heads = range(DN_HEADS)

    @pl.when(i == 0)
    def _():
        s_scr[...] = s0_ref[0]
        u_scr[...] = jnp.zeros(u_scr.shape, F32)
        wq_scr[...] = jnp.zeros(wq_scr.shape, BF16)
        attn_scr[...] = jnp.zeros(attn_scr.shape, BF16)
        kt_scr[...] = jnp.zeros(kt_scr.shape, BF16)
        gl_scr[...] = jnp.ones(gl_scr.shape, F32)

    def recurrence():
        s = [s_scr[h] for h in heads]
        for step in range(CHUNKS_PER_BLOCK):
            c = (CHUNKS_PER_BLOCK - 1 - step) if reverse else step
            r = [_dot(wq_scr[h, c], s[h].astype(BF16)) for h in heads]
            yield
            v_new = [(u_scr[h, pl.ds(c * CHUNK, CHUNK), :] - r[h][:CHUNK]).astype(BF16) for h in heads]
            for h in heads:
                o_ref[0, pl.ds(c * CHUNK, CHUNK), h * LANES:(h + 1) * LANES] = (
                    _dot(attn_scr[h, c], v_new[h]) + r[h][CHUNK:]).astype(o_ref.dtype)
            s = [_dot(kt_scr[h, c], v_new[h])
                 + s[h] * jnp.concatenate([gl_scr[h, c]] * (DN_HEAD_DIM // SUBLANES), axis=0)
                 for h in heads]
            yield
        for h in heads:
            s_scr[h] = s[h]
        while True:
            yield

    rec = recurrence()
    next(rec)

    row = lax.broadcasted_iota(jnp.int32, (nb, nb), 0)
    col = lax.broadcasted_iota(jnp.int32, (nb, nb), 1)
    same = (row // CHUNK) == (col // CHUNK)
    if reverse:
        incl = jnp.logical_and(same, col >= row)
        strict = jnp.logical_and(same, col > row)
    else:
        incl = jnp.logical_and(same, col <= row)
        strict = jnp.logical_and(same, col < row)
    base_blocks = (row // INV_BASE) == (col // INV_BASE)
    merge_masks = []
    size = INV_BASE
    while size < CHUNK:
        merge_masks.append(jnp.logical_and((row // (2 * size)) == (col // (2 * size)),
                                           (row // size) != (col // size)))
        size *= 2

    gbv = gb_ref[0]
    gc = _dot_mask_f32(jnp.where(incl, 1.0, 0.0).astype(BF16), gbv)
    tot = _dot_mask_f32(jnp.where(same, 1.0, 0.0).astype(BF16), gbv)
    gc_t = gc.T
    next(rec)

    lgs = [(DN_HEADS if reverse else 0) + h for h in heads]
    qbf = [q_ref[0, :, h * LANES:(h + 1) * LANES] for h in heads]
    kbf = [k_ref[0, :, h * LANES:(h + 1) * LANES] for h in heads]
    q = [qbf[h].astype(F32) for h in heads]
    k = [kbf[h].astype(F32) for h in heads]
    v = [v_ref[0, :, h * LANES:(h + 1) * LANES].astype(F32) for h in heads]
    gcol = [gc[:, lg:lg + 1] for lg in lgs]
    tcol = [tot[:, lg:lg + 1] for lg in lgs]
    beta = [gbv[:, 2 * DN_HEADS + lg:2 * DN_HEADS + lg + 1] for lg in lgs]
    decay = [jnp.where(incl, jnp.exp(jnp.where(incl, gcol[h] - gc_t[lgs[h]:lgs[h] + 1, :], 0.0)), 0.0)
             for h in heads]
    kb = [k[h] * beta[h] for h in heads]
    eg = [jnp.exp(gcol[h]) for h in heads]
    lfull = [_dot_nt(kb[h].astype(BF16), kbf[h]) * jnp.where(strict, decay[h], 0.0) for h in heads]
    next(rec)
    l0 = [jnp.where(base_blocks, lfull[h], 0.0) for h in heads]
    n = [-l0[h] for h in heads]
    pb = [l0[h].astype(BF16) for h in heads]
    p = [_dot(pb[h], pb[h]) for h in heads]
    next(rec)
    for it in range(3):
        pb = [p[h].astype(BF16) for h in heads]
        n = [_dot(n[h].astype(BF16), pb[h]) + (n[h] + p[h]) for h in heads]
        next(rec)
        if it < 2:
            p = [_dot(pb[h], pb[h]) for h in heads]
            next(rec)
    for pair_mask in merge_masks:
        cm = [jnp.where(pair_mask, lfull[h], 0.0) for h in heads]
        nbf = [n[h].astype(BF16) for h in heads]
        xm = [_dot(cm[h].astype(BF16), nbf[h]) + cm[h] for h in heads]
        next(rec)
        n = [n[h] - (_dot(nbf[h], xm[h].astype(BF16)) + xm[h]) for h in heads]
        next(rec)
    rhs = [jnp.concatenate([v[h] * beta[h], kb[h] * eg[h]], axis=-1) for h in heads]
    y = [_dot(n[h].astype(BF16), rhs[h].astype(BF16)) + rhs[h] for h in heads]
    next(rec)
    attn = [_dot_nt(qbf[h], kbf[h]) * decay[h] for h in heads]
    for _ in range(2 * CHUNKS_PER_BLOCK):
        next(rec)
    for h in heads:
        kt_t = (k[h] * jnp.exp(tcol[h] - gcol[h])).T
        qg = q[h] * eg[h]
        u_scr[h] = y[h][:, :LANES]
        for c in range(CHUNKS_PER_BLOCK):
            sl = slice(c * CHUNK, (c + 1) * CHUNK)
            wq_scr[h, c, 0:CHUNK, :] = y[h][sl, LANES:].astype(BF16)
            wq_scr[h, c, CHUNK:2 * CHUNK, :] = qg[sl, :].astype(BF16)
            attn_scr[h, c] = attn[h][sl, sl].astype(BF16)
            kt_scr[h, c] = kt_t[:, sl].astype(BF16)
            gl_scr[h, c] = jnp.broadcast_to(
                jnp.exp(tot[c * CHUNK:c * CHUNK + 1, lgs[h]:lgs[h] + 1]), (SUBLANES, LANES))

    @pl.when(i == nblk)
    def _():
        sfin_ref[0] = s_scr[...]


def _delta(q, k, v, gb, s0, *, reverse):
    bsz, t, _ = q.shape
    nb = DELTA_BLOCK
    nblk = t // nb
    if reverse:
        bidx = lambda i: nblk - 1 - i
    else:
        bidx = lambda i: i
    blk = lambda n: pl.BlockSpec((1, nb, n), lambda b, i: (b, bidx(jnp.minimum(i, nblk - 1)), 0))
    out_blk = pl.BlockSpec((1, nb, D_DN), lambda b, i: (b, bidx(jnp.maximum(i - 1, 0)), 0))
    state_spec = pl.BlockSpec((1, DN_HEADS, DN_HEAD_DIM, DN_HEAD_DIM), lambda b, i: (b, 0, 0, 0))
    return pl.pallas_call(
        functools.partial(_delta_kernel, reverse=reverse, nblk=nblk),
        grid=(bsz, nblk + 1),
        in_specs=[blk(D_DN), blk(D_DN), blk(D_DN), blk(LANES), state_spec],
        out_specs=[out_blk, state_spec],
        out_shape=[
            jax.ShapeDtypeStruct((bsz, t, D_DN), BF16),
            jax.ShapeDtypeStruct((bsz, DN_HEADS, DN_HEAD_DIM, DN_HEAD_DIM), F32),
        ],
        scratch_shapes=[
            pltpu.VMEM((DN_HEADS, DN_HEAD_DIM, DN_HEAD_DIM), F32),
            pltpu.VMEM((DN_HEADS, nb, LANES), F32),
            pltpu.VMEM((DN_HEADS, CHUNKS_PER_BLOCK, 2 * CHUNK, LANES), BF16),
            pltpu.VMEM((DN_HEADS, CHUNKS_PER_BLOCK, CHUNK, CHUNK), BF16),
            pltpu.VMEM((DN_HEADS, CHUNKS_PER_BLOCK, DN_HEAD_DIM, CHUNK), BF16),
            pltpu.VMEM((DN_HEADS, CHUNKS_PER_BLOCK, SUBLANES, LANES), F32),
        ],
        compiler_params=pltpu.CompilerParams(
            dimension_semantics=("arbitrary", "arbitrary"), vmem_limit_bytes=VMEM_LIMIT),
        name="delta_bwd" if reverse else "delta_fwd",
    )(q, k, v, gb, s0)


def _outproj_kernel(*refs, final):
    if final:
        (conv_ref, cgs_ref, of_ref, ob_ref, zs_ref, x_ref, cb_ref, lnw_ref, lnb_ref, dnw_ref,
         gate_ref, wc_ref, wd_ref, fnw_ref, o_ref) = refs
    else:
        (conv_ref, cgs_ref, of_ref, ob_ref, zs_ref, x_ref, cb_ref, lnw_ref, lnb_ref, dnw_ref,
         gate_ref, wc_ref, wd_ref, o_ref) = refs
    yc = conv_ref[0].astype(F32) + cb_ref[...]
    mu = jnp.mean(yc, axis=-1, keepdims=True)
    d = yc - mu
    var = jnp.mean(d * d, axis=-1, keepdims=True)
    yn = d * lax.rsqrt(var + LN_EPS) * lnw_ref[...] + lnb_ref[...]
    y_conv = (_silu(yn) * cgs_ref[0].astype(F32)).astype(BF16)
    o = of_ref[0].astype(F32) + ob_ref[0].astype(F32)
    acc = _dot(y_conv, wc_ref[...])
    for h in range(DN_HEADS):
        sl = slice(h * LANES, (h + 1) * LANES)
        oh = o[:, sl]
        on = oh * lax.rsqrt(jnp.mean(oh * oh, axis=-1, keepdims=True) + EPS) * dnw_ref[...]
        y_dn = (on * zs_ref[0, :, sl].astype(F32)).astype(BF16)
        acc = acc + _dot(y_dn, wd_ref[sl, :])
    xn = x_ref[0] + gate_ref[0] * acc
    if final:
        xn = xn * lax.rsqrt(jnp.mean(xn * xn, axis=-1, keepdims=True) + EPS) * fnw_ref[...]
    o_ref[0] = xn


def _outproj(conv, cgs, o_f, o_b, zs, x, cb, lnw, lnb, dnw, gate, wc, wd, fnw, *, tm):
    bsz, t, _ = x.shape
    final = fnw is not None
    row = lambda n: pl.BlockSpec((1, tm, n), lambda b, i: (b, i, 0))
    const = lambda shp: pl.BlockSpec(shp, lambda b, i: (0,) * len(shp))
    per_b = pl.BlockSpec((1, 1, D_MODEL), lambda b, i: (b, 0, 0))
    in_specs = [row(D_CONV), row(D_CONV), row(D_DN), row(D_DN), row(D_DN), row(D_MODEL),
                const((1, D_CONV)), const((1, D_CONV)), const((1, D_CONV)), const((1, DN_HEAD_DIM)),
                per_b, const((D_CONV, D_MODEL)), const((D_DN, D_MODEL))]
    args = [conv, cgs, o_f, o_b, zs, x, cb, lnw, lnb, dnw, gate, wc, wd]
    if final:
        in_specs.append(const((1, D_MODEL)))
        args.append(fnw)
    return pl.pallas_call(
        functools.partial(_outproj_kernel, final=final),
        grid=(bsz, t // tm),
        in_specs=in_specs,
        out_specs=row(D_MODEL),
        out_shape=jax.ShapeDtypeStruct((bsz, t, D_MODEL), F32),
        compiler_params=pltpu.CompilerParams(
            dimension_semantics=("parallel", "parallel"), vmem_limit_bytes=VMEM_LIMIT),
        name="outproj_final" if final else "outproj",
    )(*args)


def _layer_weights(l, norm_w, w_in, conv_w, conv_b, conv_ln_w, conv_ln_b, short_conv_w, a_log,
                   dt_bias, dn_norm_w, w_out):
    wi = w_in[l]
    c0 = 3 * D_CONV
    w_ab = jnp.zeros((D_MODEL, LANES), F32).at[:, :4 * DN_HEADS].set(wi[:, c0 + 4 * D_DN:])
    pad_vec = lambda v: v.reshape(2 * DN_HEADS, 1)
    return dict(
        nw=norm_w[l].reshape(1, D_MODEL),
        w_full=[wi[:, :D_CONV].astype(BF16), wi[:, D_CONV:2 * D_CONV].astype(BF16),
                wi[:, 2 * D_CONV:c0].astype(BF16), wi[:, c0 + 3 * D_DN:c0 + 4 * D_DN].astype(BF16)],
        w_dn=[wi[:, c0:c0 + 3 * D_DN].astype(BF16), w_ab.astype(BF16)],
        alog=pad_vec(a_log[l]), dtb=pad_vec(dt_bias[l]),
        conv_w=jnp.zeros((4 * SUBLANES, D_CONV), F32).at[:CONV_W].set(conv_w[l]),
        conv_b=conv_b[l].reshape(1, D_CONV), lnw=conv_ln_w[l].reshape(1, D_CONV),
        lnb=conv_ln_b[l].reshape(1, D_CONV),
        scw=jnp.zeros((SUBLANES, 3 * D_DN), F32).at[:SHORT_CONV_W].set(short_conv_w[l]),
        dnw=dn_norm_w[l].reshape(1, DN_HEAD_DIM),
        wc=w_out[l, :D_CONV].astype(BF16), wd=w_out[l, D_CONV:].astype(BF16),
    )


def _mixer(x, mod, p, s0_f, s0_b, *, grid_mode, tm, fnw=None, states_only=False):
    bsz = x.shape[0]
    shift, scale, gate = (mod[:, j * D_MODEL:(j + 1) * D_MODEL].reshape(bsz, 1, D_MODEL) for j in range(3))
    if states_only:
        q, k, v, gb = _inproj(x, p["nw"], shift, scale, p["w_dn"], p["alog"], p["dtb"], p["scw"],
                              full=False, tm=tm)
        _, s_f = _delta(q, k, v, gb, s0_f, reverse=False)
        _, s_b = _delta(q, k, v, gb, s0_b, reverse=True)
        return x, s_f, s_b
    u, cgs, zs, q, k, v, gb = _inproj(x, p["nw"], shift, scale, p["w_full"] + p["w_dn"], p["alog"],
                                      p["dtb"], p["scw"], full=True, tm=tm)
    conv = _dwconv(u, p["conv_w"], grid_mode=grid_mode)
    o_f, s_f = _delta(q, k, v, gb, s0_f, reverse=False)
    o_b, s_b = _delta(q, k, v, gb, s0_b, reverse=True)
    x_new = _outproj(conv, cgs, o_f, o_b, zs, x, p["conv_b"], p["lnw"], p["lnb"], p["dnw"], gate,
                     p["wc"], p["wd"], fnw, tm=tm)
    return x_new, s_f, s_b


def kernel(x, c, ctx, c_ctx, norm_w, w_ada, b_ada, w_in, conv_w, conv_b, conv_ln_w, conv_ln_b, short_conv_w, a_log, dt_bias, dn_norm_w, w_out, final_norm_w):
    bsz = x.shape[0]
    cond_rows = jnp.zeros((SUBLANES, D_MODEL), F32).at[:bsz].set(c).at[bsz].set(c_ctx)
    mods = _ada_all(cond_rows, w_ada, b_ada)
    zeros = jnp.zeros((bsz, DN_HEADS, DN_HEAD_DIM, DN_HEAD_DIM), F32)
    for l in range(DEPTH):
        p = _layer_weights(l, norm_w, w_in, conv_w, conv_b, conv_ln_w, conv_ln_b, short_conv_w,
                           a_log, dt_bias, dn_norm_w, w_out)
        mod_ctx = jnp.broadcast_to(mods[l, bsz:bsz + 1], (bsz, 3 * D_MODEL))
        last = l == DEPTH - 1
        ctx_new, s_f, s_b = _mixer(ctx, mod_ctx, p, zeros, zeros, grid_mode=False, tm=ctx.shape[1],
                                   states_only=last)
        x, _, _ = _mixer(x, mods[l, :bsz], p, s_f, s_b, grid_mode=True, tm=512,
                         fnw=final_norm_w.reshape(1, D_MODEL) if last else None)
        ctx = ctx_new
    return x
```

```python
import functools

import jax
import jax.numpy as jnp
from jax import lax
from jax.experimental import pallas as pl
from jax.experimental.pallas import tpu as pltpu

D_MODEL = 1024
DEPTH = 4
GRID_W = 64
D_CONV = 512
CONV_W = 31
CONV_PAD = (CONV_W - 1) // 2
DN_HEADS = 4
DN_HEAD_DIM = 128
D_DN = DN_HEADS * DN_HEAD_DIM
SHORT_CONV_W = 5
SHORT_PAD = (SHORT_CONV_W - 1) // 2
CHUNK = 64
EPS = 1e-6
LN_EPS = 1e-5

LANES = 128
SUBLANES = 8
DELTA_BLOCK = 256
CHUNKS_PER_BLOCK = DELTA_BLOCK // CHUNK
SOLVE_BLOCK = 128
INV_BASE = 16
BF16_ROWS = 16
HALO = BF16_ROWS
SEG_GAP = 16
VMEM_LIMIT = 48 * 1024 * 1024

F32 = jnp.float32
BF16 = jnp.bfloat16


def _dot(a, b):
    return jnp.dot(a, b, preferred_element_type=F32)


def _dot_nt(a, b):
    return lax.dot_general(a, b, (((1,), (1,)), ((), ())), preferred_element_type=F32)


def _sigmoid(x):
    return 0.5 * jnp.tanh(0.5 * x) + 0.5


def _silu(x):
    half = 0.5 * x
    return half * jnp.tanh(half) + half


def _softplus(x):
    return jnp.maximum(x, 0.0) + jnp.log(1.0 + jnp.exp(-jnp.abs(x)))


def _dot_mask_f32(x, mask_bf16):
    hi = x.astype(BF16)
    r1 = x - hi.astype(F32)
    mid = r1.astype(BF16)
    lo = (r1 - mid.astype(F32)).astype(BF16)
    return _dot(hi, mask_bf16) + _dot(mid, mask_bf16) + _dot(lo, mask_bf16)


def _ada_kernel(cond_ref, w_ref, b_ref, o_ref):
    s = _silu(cond_ref[...])
    o_ref[0] = jnp.dot(s, w_ref[0], precision=lax.Precision.HIGHEST,
                       preferred_element_type=F32) + b_ref[0]


def _ada_all(cond_rows, w_ada, b_ada):
    tn = 1024
    return pl.pallas_call(
        _ada_kernel,
        grid=(DEPTH, 3 * D_MODEL // tn),
        in_specs=[
            pl.BlockSpec((SUBLANES, D_MODEL), lambda l, j: (0, 0)),
            pl.BlockSpec((1, D_MODEL, tn), lambda l, j: (l, 0, j)),
            pl.BlockSpec((1, 1, tn), lambda l, j: (l, 0, j)),
        ],
        out_specs=pl.BlockSpec((1, SUBLANES, tn), lambda l, j: (l, 0, j)),
        out_shape=jax.ShapeDtypeStruct((DEPTH, SUBLANES, 3 * D_MODEL), F32),
        compiler_params=pltpu.CompilerParams(vmem_limit_bytes=VMEM_LIMIT),
        name="ada",
    )(cond_rows, w_ada, b_ada.reshape(DEPTH, 1, 3 * D_MODEL))


def _inproj_kernel(*refs, full, tm, nblk):
    if full:
        (prev_ref, x_ref, next_ref, nw_ref, shift_ref, scale_ref, wa_ref, wb_ref, wcg_ref, wz_ref,
         wqkv_ref, wab_ref, alog_ref, dtb_ref, scw_ref,
         u_ref, cgs_ref, zs_ref, q_ref, k_ref, v_ref, gb_ref, qkv_scr) = refs
    else:
        (prev_ref, x_ref, next_ref, nw_ref, shift_ref, scale_ref, wqkv_ref, wab_ref, alog_ref, dtb_ref,
         scw_ref, q_ref, k_ref, v_ref, gb_ref, qkv_scr) = refs
    i = pl.program_id(1)

    def modulated(x):
        y = x * lax.rsqrt(jnp.mean(x * x, axis=-1, keepdims=True) + EPS) * nw_ref[...]
        return (y * (1.0 + scale_ref[0]) + shift_ref[0]).astype(BF16)

    h = modulated(x_ref[0])

    h_ext = jnp.concatenate([modulated(prev_ref[0]), h, modulated(next_ref[0])], axis=0)
    rows = lax.broadcasted_iota(jnp.int32, (tm + 2 * HALO, 1), 0)
    first_row = jnp.where(i > 0, 0, HALO)
    end_row = jnp.where(i < nblk - 1, tm + 2 * HALO, tm + HALO)
    inside = jnp.logical_and(rows >= first_row, rows < end_row)
    qkv = jnp.where(inside, _dot(h_ext, wqkv_ref[...]), 0.0)
    for grp in range(3 * DN_HEADS):
        qkv_scr[grp] = qkv[:, grp * LANES:(grp + 1) * LANES]

    def conv_silu(grp):
        acc = jnp.zeros((tm, LANES), F32)
        for j in range(SHORT_CONV_W):
            acc = acc + (qkv_scr[grp, pl.ds(HALO - SHORT_PAD + j, tm), :]
                         * scw_ref[j:j + 1, grp * LANES:(grp + 1) * LANES])
        return _silu(acc)

    def l2n(m):
        return m * lax.rsqrt(jnp.sum(m * m, axis=-1, keepdims=True) + EPS)

    def qkv_epilogue(hd):
        sl = slice(hd * LANES, (hd + 1) * LANES)
        q_ref[0, :, sl] = (l2n(conv_silu(hd)) * (DN_HEAD_DIM ** -0.5)).astype(q_ref.dtype)
        k_ref[0, :, sl] = l2n(conv_silu(DN_HEADS + hd)).astype(k_ref.dtype)
        v_ref[0, :, sl] = conv_silu(2 * DN_HEADS + hd).astype(v_ref.dtype)

    if full:
        u_ref[0] = _dot(h, wa_ref[...]) * _sigmoid(_dot(h, wb_ref[...]))
        qkv_epilogue(0)
        cgs_ref[0] = _silu(_dot(h, wcg_ref[...])).astype(cgs_ref.dtype)
        qkv_epilogue(1)
        zs_ref[0] = _silu(_dot(h, wz_ref[...])).astype(zs_ref.dtype)
        qkv_epilogue(2)
    else:
        for hd in range(DN_HEADS - 1):
            qkv_epilogue(hd)
    ab_t = _dot(h, wab_ref[...]).T
    g_t = -jnp.exp(alog_ref[...]) * _softplus(ab_t[0:2 * DN_HEADS] + dtb_ref[...])
    gb_ref[0] = jnp.concatenate([g_t, _sigmoid(ab_t[2 * DN_HEADS:4 * DN_HEADS])], axis=0)
    qkv_epilogue(DN_HEADS - 1)


def _inproj(x, nw, shift, scale, wts, alog, dtb, scw, *, full, tm):
    bsz, t, _ = x.shape
    nblk = t // tm
    halo_per_blk = tm // HALO
    last_halo = t // HALO - 1
    row = lambda n: pl.BlockSpec((1, tm, n), lambda b, i: (b, i, 0))
    const = lambda shp: pl.BlockSpec(shp, lambda b, i: (0,) * len(shp))
    per_b = pl.BlockSpec((1, 1, D_MODEL), lambda b, i: (b, 0, 0))
    halo_prev = pl.BlockSpec((1, HALO, D_MODEL),
                             lambda b, i: (b, jnp.maximum(i * halo_per_blk - 1, 0), 0))
    halo_next = pl.BlockSpec((1, HALO, D_MODEL),
                             lambda b, i: (b, jnp.minimum((i + 1) * halo_per_blk, last_halo), 0))
    w_specs = [const(w.shape) for w in wts]
    in_specs = ([halo_prev, row(D_MODEL), halo_next, const((1, D_MODEL)), per_b, per_b] + w_specs
                + [const((2 * DN_HEADS, 1))] * 2 + [const((SUBLANES, 3 * D_DN))])
    outs = ([(D_CONV, F32), (D_CONV, BF16), (D_DN, BF16)] if full else []) + [
        (D_DN, BF16), (D_DN, BF16), (D_DN, BF16)]
    gb_spec = pl.BlockSpec((1, 4 * DN_HEADS, tm), lambda b, i: (b, 0, i))
    gb_shape = jax.ShapeDtypeStruct((bsz, 4 * DN_HEADS, t), F32)
    return pl.pallas_call(
        functools.partial(_inproj_kernel, full=full, tm=tm, nblk=nblk),
        grid=(bsz, nblk),
        in_specs=in_specs,
        out_specs=[row(n) for n, _ in outs] + [gb_spec],
        out_shape=[jax.ShapeDtypeStruct((bsz, t, n), dt) for n, dt in outs] + [gb_shape],
        scratch_shapes=[pltpu.VMEM((3 * DN_HEADS, tm + 2 * HALO, LANES), F32)],
        compiler_params=pltpu.CompilerParams(
            dimension_semantics=("parallel", "parallel"), vmem_limit_bytes=VMEM_LIMIT),
        name="inproj_full" if full else "inproj_dn",
    )(x, x, x, nw, shift, scale, *wts, alog, dtb, scw)


def _conv_segments(u_ref, w_ref, o_ref, pad_ref, *, seg, nseg):
    stride = seg + SEG_GAP
    zeros_gap = jnp.zeros((SEG_GAP, LANES), F32)

    def fill(r, carry):
        base = pl.multiple_of(r * stride, SUBLANES)
        src = pl.multiple_of(r * seg, SUBLANES)
        pad_ref[pl.ds(base, SEG_GAP), :] = zeros_gap
        pad_ref[pl.ds(base + SEG_GAP, seg), :] = u_ref[0, pl.ds(src, seg), :]
        return carry

    lax.fori_loop(0, nseg, fill, 0)
    pad_ref[pl.ds(nseg * stride, SEG_GAP), :] = zeros_gap

    def body(r, carry):
        base = pl.multiple_of(r * stride, SUBLANES)
        dst = pl.multiple_of(r * seg, BF16_ROWS)
        acc = jnp.zeros((seg, LANES), F32)
        for k in range(CONV_W):
            acc = acc + pad_ref[pl.ds(base + (SEG_GAP - CONV_PAD) + k, seg), :] * w_ref[k:k + 1, :]
        o_ref[0, pl.ds(dst, seg), :] = acc.astype(o_ref.dtype)
        return carry

    lax.fori_loop(0, nseg, body, 0)


def _conv_strided(u_ref, w_ref, o_ref, pad_ref, *, t, step):
    halo = CONV_PAD * step
    pad_ref[pl.ds(0, halo), :] = jnp.zeros((halo, LANES), F32)
    pad_ref[pl.ds(halo + t, halo), :] = jnp.zeros((halo, LANES), F32)
    pad_ref[pl.ds(halo, t), :] = u_ref[0]

    def body(r, carry):
        base = pl.multiple_of(r * step, BF16_ROWS)
        acc = jnp.zeros((step, LANES), F32)
        for k in range(CONV_W):
            acc = acc + pad_ref[pl.ds(base + k * step, step), :] * w_ref[k:k + 1, :]
        o_ref[0, pl.ds(base, step), :] = acc.astype(o_ref.dtype)
        return carry

    lax.fori_loop(0, t // step, body, 0)


def _grid_conv_kernel(u_ref, w_ref, o_ref, pad_ref, *, t):
    j = pl.program_id(1)
    half_blocks = D_CONV // LANES // 2

    @pl.when(j < half_blocks)
    def _():
        _conv_segments(u_ref, w_ref, o_ref, pad_ref, seg=GRID_W, nseg=t // GRID_W)

    @pl.when(j >= half_blocks)
    def _():
        _conv_strided(u_ref, w_ref, o_ref, pad_ref, t=t, step=GRID_W)


def _seq_conv_kernel(u_ref, w_ref, o_ref, pad_ref, *, t):
    _conv_segments(u_ref, w_ref, o_ref, pad_ref, seg=t, nseg=1)


def _dwconv(u, w_pad, *, grid_mode):
    bsz, t, _ = u.shape
    if grid_mode:
        rows = t // GRID_W
        pad_rows = max(rows * (GRID_W + SEG_GAP) + SEG_GAP, t + 2 * CONV_PAD * GRID_W)
        body = functools.partial(_grid_conv_kernel, t=t)
    else:
        pad_rows = t + 2 * SEG_GAP
        body = functools.partial(_seq_conv_kernel, t=t)
    return pl.pallas_call(
        body,
        grid=(bsz, D_CONV // LANES),
        in_specs=[
            pl.BlockSpec((1, t, LANES), lambda b, j: (b, 0, j)),
            pl.BlockSpec((4 * SUBLANES, LANES), lambda b, j: (0, j)),
        ],
        out_specs=pl.BlockSpec((1, t, LANES), lambda b, j: (b, 0, j)),
        out_shape=jax.ShapeDtypeStruct((bsz, t, D_CONV), BF16),
        scratch_shapes=[pltpu.VMEM((pad_rows, LANES), F32)],
        compiler_params=pltpu.CompilerParams(
            dimension_semantics=("parallel", "parallel"), vmem_limit_bytes=VMEM_LIMIT),
        name="grid_conv" if grid_mode else "seq_conv",
    )(u, w_pad)


def _delta_kernel(q_ref, k_ref, v_ref, gb_ref, s0_ref, o_ref, sfin_ref,
                  s_scr, u_scr, wq_scr, ak_scr, gl_scr, *, reverse, nblk):
    nb = DELTA_BLOCK
    i = pl.program_id(1)
    heads = range(DN_HEADS)

    @pl.when(i == 0)
    def _():
        s_scr[...] = s0_ref[0]
        u_scr[...] = jnp.zeros(u_scr.shape, F32)
        wq_scr[...] = jnp.zeros(wq_scr.shape, BF16)
        ak_scr[...] = jnp.zeros(ak_scr.shape, BF16)
        gl_scr[...] = jnp.ones(gl_scr.shape, F32)

    def recurrence():
        s = [s_scr[h] for h in heads]
        for step in range(CHUNKS_PER_BLOCK):
            c = (CHUNKS_PER_BLOCK - 1 - step) if reverse else step
            r = [_dot(wq_scr[h, c], s[h].astype(BF16)) for h in heads]
            yield
            v_new = [(u_scr[h, pl.ds(c * CHUNK, CHUNK), :] - r[h][:CHUNK]).astype(BF16) for h in heads]
            av = [_dot(ak_scr[h, c], v_new[h]) for h in heads]
            for h in heads:
                o_ref[0, pl.ds(c * CHUNK, CHUNK), h * LANES:(h + 1) * LANES] = (
                    av[h][:CHUNK] + r[h][CHUNK:]).astype(o_ref.dtype)
            s = [av[h][CHUNK:]
                 + s[h] * jnp.concatenate([gl_scr[h, c]] * (DN_HEAD_DIM // SUBLANES), axis=0)
                 for h in heads]
            yield
        for h in heads:
            s_scr[h] = s[h]
        while True:
            yield

    rec = recurrence()
    next(rec)

    sb = SOLVE_BLOCK
    units = [(h, hf) for h in heads for hf in range(nb // sb)]
    nu = range(len(units))
    lgs = [(DN_HEADS if reverse else 0) + h for h, _ in units]
    tok = [slice(hf * sb, (hf + 1) * sb) for _, hf in units]
    qbf = [q_ref[0, tok[u], h * LANES:(h + 1) * LANES] for u, (h, _) in enumerate(units)]
    kbf = [k_ref[0, tok[u], h * LANES:(h + 1) * LANES] for u, (h, _) in enumerate(units)]

    def chunk_masks(n_tok):
        row = lax.broadcasted_iota(jnp.int32, (n_tok, n_tok), 0)
        col = lax.broadcasted_iota(jnp.int32, (n_tok, n_tok), 1)
        same = (row // CHUNK) == (col // CHUNK)
        before = (col > row) if reverse else (col < row)
        return row, col, same, before

    _, _, same_nb, before_nb = chunk_masks(nb)
    gbt = gb_ref[0]
    nrow = 4 * DN_HEADS
    cum_mask = jnp.logical_and(same_nb, jnp.logical_not(before_nb))
    gc_t = _dot_mask_f32(gbt, jnp.where(cum_mask, 1.0, 0.0).astype(BF16))
    tot_t = _dot_mask_f32(gbt, jnp.where(same_nb, 1.0, 0.0).astype(BF16))
    cols = jnp.concatenate([gbt, gc_t, tot_t, jnp.zeros((LANES - 3 * nrow, nb), F32)], axis=0).T
    next(rec)

    row, col, same, before = chunk_masks(sb)
    strict = jnp.logical_and(same, before)
    incl = jnp.logical_or(strict, row == col)
    base_blocks = (row // INV_BASE) == (col // INV_BASE)

    q = [qbf[u].astype(F32) for u in nu]
    k = [kbf[u].astype(F32) for u in nu]
    v = [v_ref[0, tok[u], h * LANES:(h + 1) * LANES].astype(F32) for u, (h, _) in enumerate(units)]
    beta = [cols[tok[u], 2 * DN_HEADS + lgs[u]:2 * DN_HEADS + lgs[u] + 1] for u in nu]
    gcol = [cols[tok[u], nrow + lgs[u]:nrow + lgs[u] + 1] for u in nu]
    tcol = [cols[tok[u], 2 * nrow + lgs[u]:2 * nrow + lgs[u] + 1] for u in nu]
    decay = [jnp.where(incl, jnp.exp(jnp.where(incl, gcol[u] - gc_t[lgs[u]:lgs[u] + 1, tok[u]], 0.0)), 0.0)
             for u in nu]
    kb = [k[u] * beta[u] for u in nu]
    eg = [jnp.exp(gcol[u]) for u in nu]
    lfull = [_dot_nt(kb[u].astype(BF16), kbf[u]) * jnp.where(strict, decay[u], 0.0) for u in nu]
    next(rec)
    l0 = [jnp.where(base_blocks, lfull[u], 0.0) for u in nu]
    n = [-l0[u] for u in nu]
    pb = [l0[u].astype(BF16) for u in nu]
    p = [_dot(pb[u], pb[u]) for u in nu]
    next(rec)
    for it in range(3):
        pb = [p[u].astype(BF16) for u in nu]
        n = [_dot(n[u].astype(BF16), pb[u]) + (n[u] + p[u]) for u in nu]
        next(rec)
        if it < 2:
            p = [_dot(pb[u], pb[u]) for u in nu]
            next(rec)
    size = INV_BASE
    while size < CHUNK:
        picked = [j for j in range(sb // size) if j % 2 == (0 if reverse else 1)]

        def take(m):
            return jnp.concatenate([m[j * size:(j + 1) * size] for j in picked], axis=0)

        def put(full_blocks, picked_rows):
            blocks = list(full_blocks)
            for idx, j in enumerate(picked):
                blocks[j] = picked_rows[idx * size:(idx + 1) * size]
            return jnp.concatenate(blocks, axis=0)

        prow = lax.broadcasted_iota(jnp.int32, (sb // 2, sb), 0)
        pcol = lax.broadcasted_iota(jnp.int32, (sb // 2, sb), 1)
        orig = (prow // size) * (2 * size) + (0 if reverse else size) + prow % size
        pair_mask = jnp.logical_and(orig // (2 * size) == pcol // (2 * size), orig // size != pcol // size)
        zero_blocks = [jnp.zeros((size, sb), BF16)] * (sb // size)
        cm = [jnp.where(pair_mask, take(lfull[u]), 0.0) for u in nu]
        nbf = [n[u].astype(BF16) for u in nu]
        xm = [_dot(cm[u].astype(BF16), nbf[u]) + cm[u] for u in nu]
        next(rec)
        ym = [_dot(take(nbf[u]), put(zero_blocks, xm[u].astype(BF16))) + xm[u] for u in nu]
        n = [put([n[u][j * size:(j + 1) * size] for j in range(sb // size)], take(n[u]) - ym[u])
             for u in nu]
        next(rec)
        size *= 2
    rhs = [jnp.concatenate([v[u] * beta[u], kb[u] * eg[u]], axis=-1) for u in nu]
    y = [_dot(n[u].astype(BF16), rhs[u].astype(BF16)) + rhs[u] for u in nu]
    next(rec)
    attn = [_dot_nt(qbf[u], kbf[u]) * decay[u] for u in nu]
    for _ in range(2 * CHUNKS_PER_BLOCK):
        next(rec)
    for u, (h, hf) in enumerate(units):
        kt_t = (k[u] * jnp.exp(tcol[u] - gcol[u])).T
        qg = q[u] * eg[u]
        u_scr[h, pl.ds(hf * sb, sb), :] = y[u][:, :LANES]
        for cc in range(sb // CHUNK):
            c = hf * (sb // CHUNK) + cc
            sl = slice(cc * CHUNK, (cc + 1) * CHUNK)
            wq_scr[h, c, 0:CHUNK, :] = y[u][sl, LANES:].astype(BF16)
            wq_scr[h, c, CHUNK:2 * CHUNK, :] = qg[sl, :].astype(BF16)
            ak_scr[h, c, 0:CHUNK, :] = attn[u][sl, sl].astype(BF16)
            ak_scr[h, c, CHUNK:, :] = kt_t[:, sl].astype(BF16)
            gl_scr[h, c] = jnp.broadcast_to(
                jnp.exp(tot_t[lgs[u]:lgs[u] + 1, c * CHUNK:c * CHUNK + 1]), (SUBLANES, LANES))

    @pl.when(i == nblk)
    def _():
        sfin_ref[0] = s_scr[...]


def _delta(q, k, v, gb, s0, *, reverse):
    bsz, t, _ = q.shape
    nb = DELTA_BLOCK
    nblk = t // nb
    if reverse:
        bidx = lambda i: nblk - 1 - i
    else:
        bidx = lambda i: i
    blk = lambda n: pl.BlockSpec((1, nb, n), lambda b, i: (b, bidx(jnp.minimum(i, nblk - 1)), 0))
    gb_blk = pl.BlockSpec((1, 4 * DN_HEADS, nb), lambda b, i: (b, 0, bidx(jnp.minimum(i, nblk - 1))))
    out_blk = pl.BlockSpec((1, nb, D_DN), lambda b, i: (b, bidx(jnp.maximum(i - 1, 0)), 0))
    state_spec = pl.BlockSpec((1, DN_HEADS, DN_HEAD_DIM, DN_HEAD_DIM), lambda b, i: (b, 0, 0, 0))
    return pl.pallas_call(
        functools.partial(_delta_kernel, reverse=reverse, nblk=nblk),
        grid=(bsz, nblk + 1),
        in_specs=[blk(D_DN), blk(D_DN), blk(D_DN), gb_blk, state_spec],
        out_specs=[out_blk, state_spec],
        out_shape=[
            jax.ShapeDtypeStruct((bsz, t, D_DN), BF16),
            jax.ShapeDtypeStruct((bsz, DN_HEADS, DN_HEAD_DIM, DN_HEAD_DIM), F32),
        ],
        scratch_shapes=[
            pltpu.VMEM((DN_HEADS, DN_HEAD_DIM, DN_HEAD_DIM), F32),
            pltpu.VMEM((DN_HEADS, nb, LANES), F32),
            pltpu.VMEM((DN_HEADS, CHUNKS_PER_BLOCK, 2 * CHUNK, LANES), BF16),
            pltpu.VMEM((DN_HEADS, CHUNKS_PER_BLOCK, CHUNK + DN_HEAD_DIM, CHUNK), BF16),
            pltpu.VMEM((DN_HEADS, CHUNKS_PER_BLOCK, SUBLANES, LANES), F32),
        ],
        compiler_params=pltpu.CompilerParams(
            dimension_semantics=("arbitrary", "arbitrary"), vmem_limit_bytes=VMEM_LIMIT),
        name="delta_bwd" if reverse else "delta_fwd",
    )(q, k, v, gb, s0)


def _outproj_kernel(*refs, final):
    if final:
        (conv_ref, cgs_ref, of_ref, ob_ref, zs_ref, x_ref, cb_ref, lnw_ref, lnb_ref, dnw_ref,
         gate_ref, wc_ref, wd_ref, fnw_ref, o_ref) = refs
    else:
        (conv_ref, cgs_ref, of_ref, ob_ref, zs_ref, x_ref, cb_ref, lnw_ref, lnb_ref, dnw_ref,
         gate_ref, wc_ref, wd_ref, o_ref) = refs
    yc = conv_ref[0].astype(F32) + cb_ref[...]
    mu = jnp.mean(yc, axis=-1, keepdims=True)
    d = yc - mu
    var = jnp.mean(d * d, axis=-1, keepdims=True)
    yn = d * lax.rsqrt(var + LN_EPS) * lnw_ref[...] + lnb_ref[...]
    y_conv = (_silu(yn) * cgs_ref[0].astype(F32)).astype(BF16)
    o = of_ref[0].astype(F32) + ob_ref[0].astype(F32)
    acc = _dot(y_conv, wc_ref[...])
    for h in range(DN_HEADS):
        sl = slice(h * LANES, (h + 1) * LANES)
        oh = o[:, sl]
        on = oh * lax.rsqrt(jnp.mean(oh * oh, axis=-1, keepdims=True) + EPS) * dnw_ref[...]
        y_dn = (on * zs_ref[0, :, sl].astype(F32)).astype(BF16)
        acc = acc + _dot(y_dn, wd_ref[sl, :])
    xn = x_ref[0] + gate_ref[0] * acc
    if final:
        xn = xn * lax.rsqrt(jnp.mean(xn * xn, axis=-1, keepdims=True) + EPS) * fnw_ref[...]
    o_ref[0] = xn


def _outproj(conv, cgs, o_f, o_b, zs, x, cb, lnw, lnb, dnw, gate, wc, wd, fnw, *, tm):
    bsz, t, _ = x.shape
    final = fnw is not None
    row = lambda n: pl.BlockSpec((1, tm, n), lambda b, i: (b, i, 0))
    const = lambda shp: pl.BlockSpec(shp, lambda b, i: (0,) * len(shp))
    per_b = pl.BlockSpec((1, 1, D_MODEL), lambda b, i: (b, 0, 0))
    in_specs = [row(D_CONV), row(D_CONV), row(D_DN), row(D_DN), row(D_DN), row(D_MODEL),
                const((1, D_CONV)), const((1, D_CONV)), const((1, D_CONV)), const((1, DN_HEAD_DIM)),
                per_b, const((D_CONV, D_MODEL)), const((D_DN, D_MODEL))]
    args = [conv, cgs, o_f, o_b, zs, x, cb, lnw, lnb, dnw, gate, wc, wd]
    if final:
        in_specs.append(const((1, D_MODEL)))
        args.append(fnw)
    return pl.pallas_call(
        functools.partial(_outproj_kernel, final=final),
        grid=(bsz, t // tm),
        in_specs=in_specs,
        out_specs=row(D_MODEL),
        out_shape=jax.ShapeDtypeStruct((bsz, t, D_MODEL), F32),
        compiler_params=pltpu.CompilerParams(
            dimension_semantics=("parallel", "parallel"), vmem_limit_bytes=VMEM_LIMIT),
        name="outproj_final" if final else "outproj",
    )(*args)


def _layer_weights(l, norm_w, w_in, conv_w, conv_b, conv_ln_w, conv_ln_b, short_conv_w, a_log,
                   dt_bias, dn_norm_w, w_out):
    wi = w_in[l]
    c0 = 3 * D_CONV
    w_ab = jnp.zeros((D_MODEL, LANES), F32).at[:, :4 * DN_HEADS].set(wi[:, c0 + 4 * D_DN:])
    pad_vec = lambda v: v.reshape(2 * DN_HEADS, 1)
    return dict(
        nw=norm_w[l].reshape(1, D_MODEL),
        w_full=[wi[:, :D_CONV].astype(BF16), wi[:, D_CONV:2 * D_CONV].astype(BF16),
                wi[:, 2 * D_CONV:c0].astype(BF16), wi[:, c0 + 3 * D_DN:c0 + 4 * D_DN].astype(BF16)],
        w_dn=[wi[:, c0:c0 + 3 * D_DN].astype(BF16), w_ab.astype(BF16)],
        alog=pad_vec(a_log[l]), dtb=pad_vec(dt_bias[l]),
        conv_w=jnp.zeros((4 * SUBLANES, D_CONV), F32).at[:CONV_W].set(conv_w[l]),
        conv_b=conv_b[l].reshape(1, D_CONV), lnw=conv_ln_w[l].reshape(1, D_CONV),
        lnb=conv_ln_b[l].reshape(1, D_CONV),
        scw=jnp.zeros((SUBLANES, 3 * D_DN), F32).at[:SHORT_CONV_W].set(short_conv_w[l]),
        dnw=dn_norm_w[l].reshape(1, DN_HEAD_DIM),
        wc=w_out[l, :D_CONV].astype(BF16), wd=w_out[l, D_CONV:].astype(BF16),
    )


def _mixer(x, mod, p, s0_f, s0_b, *, grid_mode, tm, fnw=None, states_only=False):
    bsz = x.shape[0]
    shift, scale, gate = (mod[:, j * D_MODEL:(j + 1) * D_MODEL].reshape(bsz, 1, D_MODEL) for j in range(3))
    if states_only:
        q, k, v, gb = _inproj(x, p["nw"], shift, scale, p["w_dn"], p["alog"], p["dtb"], p["scw"],
                              full=False, tm=tm)
        _, s_f = _delta(q, k, v, gb, s0_f, reverse=False)
        _, s_b = _delta(q, k, v, gb, s0_b, reverse=True)
        return x, s_f, s_b
    u, cgs, zs, q, k, v, gb = _inproj(x, p["nw"], shift, scale, p["w_full"] + p["w_dn"], p["alog"],
                                      p["dtb"], p["scw"], full=True, tm=tm)
    conv = _dwconv(u, p["conv_w"], grid_mode=grid_mode)
    o_f, s_f = _delta(q, k, v, gb, s0_f, reverse=False)
    o_b, s_b = _delta(q, k, v, gb, s0_b, reverse=True)
    x_new = _outproj(conv, cgs, o_f, o_b, zs, x, p["conv_b"], p["lnw"], p["lnb"], p["dnw"], gate,
                     p["wc"], p["wd"], fnw, tm=tm)
    return x_new, s_f, s_b


def kernel(x, c, ctx, c_ctx, norm_w, w_ada, b_ada, w_in, conv_w, conv_b, conv_ln_w, conv_ln_b, short_conv_w, a_log, dt_bias, dn_norm_w, w_out, final_norm_w):
    bsz = x.shape[0]
    cond_rows = jnp.zeros((SUBLANES, D_MODEL), F32).at[:bsz].set(c).at[bsz].set(c_ctx)
    mods = _ada_all(cond_rows, w_ada, b_ada)
    zeros = jnp.zeros((bsz, DN_HEADS, DN_HEAD_DIM, DN_HEAD_DIM), F32)
    for l in range(DEPTH):
        p = _layer_weights(l, norm_w, w_in, conv_w, conv_b, conv_ln_w, conv_ln_b, short_conv_w,
                           a_log, dt_bias, dn_norm_w, w_out)
        mod_ctx = jnp.broadcast_to(mods[l, bsz:bsz + 1], (bsz, 3 * D_MODEL))
        last = l == DEPTH - 1
        ctx_new, s_f, s_b = _mixer(ctx, mod_ctx, p, zeros, zeros, grid_mode=False, tm=ctx.shape[1],
                                   states_only=last)
        x, _, _ = _mixer(x, mods[l, :bsz], p, s_f, s_b, grid_mode=True, tm=512,
                         fnw=final_norm_w.reshape(1, D_MODEL) if last else None)
        ctx = ctx_new
    return x
```

```python
import functools

import jax
import jax.numpy as jnp
from jax import lax
from jax.experimental import pallas as pl
from jax.experimental.pallas import tpu as pltpu

D_MODEL = 1024
DEPTH = 4
GRID_W = 64
D_CONV = 512
CONV_W = 31
CONV_PAD = (CONV_W - 1) // 2
DN_HEADS = 4
DN_HEAD_DIM = 128
D_DN = DN_HEADS * DN_HEAD_DIM
SHORT_CONV_W = 5
SHORT_PAD = (SHORT_CONV_W - 1) // 2
CHUNK = 64
EPS = 1e-6
LN_EPS = 1e-5

LANES = 128
SUBLANES = 8
OUT_TILE = 1024
DELTA_BLOCK = 512
SOLVE_BLOCK = 128
INV_BASE = 16
BF16_ROWS = 16
HALO = BF16_ROWS
CONV_UNROLL = 4
SEG_GAP = 16
VMEM_LIMIT = 48 * 1024 * 1024

F32 = jnp.float32
BF16 = jnp.bfloat16


def _dot(a, b):
    return jnp.dot(a, b, preferred_element_type=F32)


def _dot_nt(a, b):
    return lax.dot_general(a, b, (((1,), (1,)), ((), ())), preferred_element_type=F32)


def _sigmoid(x):
    return 0.5 * jnp.tanh(0.5 * x) + 0.5


def _silu(x):
    half = 0.5 * x
    return half * jnp.tanh(half) + half


def _softplus(x):
    return jnp.maximum(x, 0.0) + jnp.log(1.0 + jnp.exp(-jnp.abs(x)))


def _dot_mask_f32(x, mask_bf16):
    hi = x.astype(BF16)
    r1 = x - hi.astype(F32)
    mid = r1.astype(BF16)
    lo = (r1 - mid.astype(F32)).astype(BF16)
    return _dot(hi, mask_bf16) + _dot(mid, mask_bf16) + _dot(lo, mask_bf16)


def _ada_kernel(cond_t_ref, w_ref, b_ref, o_ref, *, nrows):
    s_t = _silu(cond_t_ref[...])
    w = w_ref[0]
    rows = [jnp.sum(s_t[:, r:r + 1] * w, axis=0, keepdims=True) for r in range(nrows)]
    rows.append(jnp.zeros((SUBLANES - nrows, w.shape[1]), F32))
    o_ref[0] = jnp.concatenate(rows, axis=0) + b_ref[0]


def _ada_all(cond_t, w_ada, b_ada, *, nrows):
    tn = 1024
    return pl.pallas_call(
        functools.partial(_ada_kernel, nrows=nrows),
        grid=(DEPTH, 3 * D_MODEL // tn),
        in_specs=[
            pl.BlockSpec((D_MODEL, LANES), lambda l, j: (0, 0)),
            pl.BlockSpec((1, D_MODEL, tn), lambda l, j: (l, 0, j)),
            pl.BlockSpec((1, 1, tn), lambda l, j: (l, 0, j)),
        ],
        out_specs=pl.BlockSpec((1, SUBLANES, tn), lambda l, j: (l, 0, j)),
        out_shape=jax.ShapeDtypeStruct((DEPTH, SUBLANES, 3 * D_MODEL), F32),
        compiler_params=pltpu.CompilerParams(vmem_limit_bytes=VMEM_LIMIT),
        name="ada",
    )(cond_t, w_ada, b_ada.reshape(DEPTH, 1, 3 * D_MODEL))


def _inproj_kernel(*refs, full, tm, nblk):
    if full:
        (prev_ref, x_ref, next_ref, nw_ref, shift_ref, scale_ref, wa_ref, wb_ref, wcg_ref, wz_ref,
         wqkv_ref, wab_ref, alog_ref, dtb_ref, scw_ref,
         u_ref, cgs_ref, zs_ref, q_ref, k_ref, v_ref, gb_ref, h_scr, qkv_scr) = refs
    else:
        (prev_ref, x_ref, next_ref, nw_ref, shift_ref, scale_ref, wqkv_ref, wab_ref, alog_ref, dtb_ref,
         scw_ref, q_ref, k_ref, v_ref, gb_ref, h_scr, qkv_scr) = refs
    i = pl.program_id(1)

    def modulated(x):
        y = x * lax.rsqrt(jnp.mean(x * x, axis=-1, keepdims=True) + EPS) * nw_ref[...]
        return (y * (1.0 + scale_ref[0]) + shift_ref[0]).astype(BF16)

    h_scr[pl.ds(0, HALO), :] = modulated(prev_ref[0])
    h_scr[pl.ds(HALO, tm), :] = modulated(x_ref[0])
    h_scr[pl.ds(HALO + tm, HALO), :] = modulated(next_ref[0])

    def h_dot(w_ref):
        return _dot(h_scr[pl.ds(HALO, tm), :], w_ref[...])

    rows = lax.broadcasted_iota(jnp.int32, (tm + 2 * HALO, 1), 0)
    first_row = jnp.where(i > 0, 0, HALO)
    end_row = jnp.where(i < nblk - 1, tm + 2 * HALO, tm + HALO)
    inside = jnp.logical_and(rows >= first_row, rows < end_row)
    qkv = jnp.where(inside, _dot(h_scr[...], wqkv_ref[...]), 0.0)
    for grp in range(3 * DN_HEADS):
        qkv_scr[grp] = qkv[:, grp * LANES:(grp + 1) * LANES]

    def conv_silu(grp):
        acc = jnp.zeros((tm, LANES), F32)
        for j in range(SHORT_CONV_W):
            acc = acc + (qkv_scr[grp, pl.ds(HALO - SHORT_PAD + j, tm), :]
                         * scw_ref[j:j + 1, grp * LANES:(grp + 1) * LANES])
        return _silu(acc)

    def l2n(m):
        return m * lax.rsqrt(jnp.sum(m * m, axis=-1, keepdims=True) + EPS)

    def qkv_epilogue(hd):
        sl = slice(hd * LANES, (hd + 1) * LANES)
        q_ref[0, :, sl] = (l2n(conv_silu(hd)) * (DN_HEAD_DIM ** -0.5)).astype(q_ref.dtype)
        k_ref[0, :, sl] = l2n(conv_silu(DN_HEADS + hd)).astype(k_ref.dtype)
        v_ref[0, :, sl] = conv_silu(2 * DN_HEADS + hd).astype(v_ref.dtype)

    if full:
        u_ref[0] = h_dot(wa_ref) * _sigmoid(h_dot(wb_ref))
        qkv_epilogue(0)
        cgs_ref[0] = _silu(h_dot(wcg_ref)).astype(cgs_ref.dtype)
        qkv_epilogue(1)
        zs_ref[0] = _silu(h_dot(wz_ref)).astype(zs_ref.dtype)
        qkv_epilogue(2)
    else:
        for hd in range(DN_HEADS - 1):
            qkv_epilogue(hd)
    ab_t = h_dot(wab_ref).T
    g_t = -jnp.exp(alog_ref[...]) * _softplus(ab_t[0:2 * DN_HEADS] + dtb_ref[...])
    gb_ref[0] = jnp.concatenate([g_t, _sigmoid(ab_t[2 * DN_HEADS:4 * DN_HEADS])], axis=0)
    qkv_epilogue(DN_HEADS - 1)


def _inproj(x, nw, shift, scale, wts, alog, dtb, scw, *, full, tm):
    bsz, t, _ = x.shape
    nblk = t // tm
    halo_per_blk = tm // HALO
    last_halo = t // HALO - 1
    row = lambda n: pl.BlockSpec((1, tm, n), lambda b, i: (b, i, 0))
    const = lambda shp: pl.BlockSpec(shp, lambda b, i: (0,) * len(shp))
    per_b = pl.BlockSpec((1, 1, D_MODEL), lambda b, i: (b, 0, 0))
    halo_prev = pl.BlockSpec((1, HALO, D_MODEL),
                             lambda b, i: (b, jnp.maximum(i * halo_per_blk - 1, 0), 0))
    halo_next = pl.BlockSpec((1, HALO, D_MODEL),
                             lambda b, i: (b, jnp.minimum((i + 1) * halo_per_blk, last_halo), 0))
    w_specs = [const(w.shape) for w in wts]
    in_specs = ([halo_prev, row(D_MODEL), halo_next, const((1, D_MODEL)), per_b, per_b] + w_specs
                + [const((2 * DN_HEADS, 1))] * 2 + [const((SUBLANES, 3 * D_DN))])
    outs = ([(D_CONV, F32), (D_CONV, BF16), (D_DN, BF16)] if full else []) + [
        (D_DN, BF16), (D_DN, BF16), (D_DN, BF16)]
    gb_spec = pl.BlockSpec((1, 4 * DN_HEADS, tm), lambda b, i: (b, 0, i))
    gb_shape = jax.ShapeDtypeStruct((bsz, 4 * DN_HEADS, t), F32)
    return pl.pallas_call(
        functools.partial(_inproj_kernel, full=full, tm=tm, nblk=nblk),
        grid=(bsz, nblk),
        in_specs=in_specs,
        out_specs=[row(n) for n, _ in outs] + [gb_spec],
        out_shape=[jax.ShapeDtypeStruct((bsz, t, n), dt) for n, dt in outs] + [gb_shape],
        scratch_shapes=[pltpu.VMEM((tm + 2 * HALO, D_MODEL), BF16),
                        pltpu.VMEM((3 * DN_HEADS, tm + 2 * HALO, LANES), F32)],
        compiler_params=pltpu.CompilerParams(
            dimension_semantics=("parallel", "parallel"), vmem_limit_bytes=VMEM_LIMIT),
        name="inproj_full" if full else "inproj_dn",
    )(x, x, x, nw, shift, scale, *wts, alog, dtb, scw)


def _conv_segments(u_ref, w_ref, o_ref, pad_ref, *, seg, nseg):
    stride = seg + SEG_GAP
    zeros_gap = jnp.zeros((SEG_GAP, LANES), F32)

    def fill(r, carry):
        base = pl.multiple_of(r * stride, SUBLANES)
        src = pl.multiple_of(r * seg, SUBLANES)
        pad_ref[pl.ds(base, SEG_GAP), :] = zeros_gap
        pad_ref[pl.ds(base + SEG_GAP, seg), :] = u_ref[0, pl.ds(src, seg), :]
        return carry

    lax.fori_loop(0, nseg, fill, 0)
    pad_ref[pl.ds(nseg * stride, SEG_GAP), :] = zeros_gap

    def body(r, carry):
        base = pl.multiple_of(r * stride, SUBLANES)
        dst = pl.multiple_of(r * seg, BF16_ROWS)
        acc = jnp.zeros((seg, LANES), F32)
        for k in range(CONV_W):
            acc = acc + pad_ref[pl.ds(base + (SEG_GAP - CONV_PAD) + k, seg), :] * w_ref[k:k + 1, :]
        o_ref[0, pl.ds(dst, seg), :] = acc.astype(o_ref.dtype)
        return carry

    lax.fori_loop(0, nseg, body, 0, unroll=CONV_UNROLL if nseg % CONV_UNROLL == 0 else 1)


def _conv_strided(u_ref, w_ref, o_ref, pad_ref, *, t, step):
    halo = CONV_PAD * step
    pad_ref[pl.ds(0, halo), :] = jnp.zeros((halo, LANES), F32)
    pad_ref[pl.ds(halo + t, halo), :] = jnp.zeros((halo, LANES), F32)
    pad_ref[pl.ds(halo, t), :] = u_ref[0]

    def body(r, carry):
        base = pl.multiple_of(r * step, BF16_ROWS)
        acc = jnp.zeros((step, LANES), F32)
        for k in range(CONV_W):
            acc = acc + pad_ref[pl.ds(base + k * step, step), :] * w_ref[k:k + 1, :]
        o_ref[0, pl.ds(base, step), :] = acc.astype(o_ref.dtype)
        return carry

    lax.fori_loop(0, t // step, body, 0, unroll=CONV_UNROLL if (t // step) % CONV_UNROLL == 0 else 1)


def _grid_conv_kernel(u_ref, w_ref, o_ref, pad_ref, *, t):
    j = pl.program_id(1)
    half_blocks = D_CONV // LANES // 2

    @pl.when(j < half_blocks)
    def _():
        _conv_segments(u_ref, w_ref, o_ref, pad_ref, seg=GRID_W, nseg=t // GRID_W)

    @pl.when(j >= half_blocks)
    def _():
        _conv_strided(u_ref, w_ref, o_ref, pad_ref, t=t, step=GRID_W)


def _seq_conv_kernel(u_ref, w_ref, o_ref, pad_ref, *, t):
    _conv_segments(u_ref, w_ref, o_ref, pad_ref, seg=t, nseg=1)


def _dwconv(u, w_pad, *, grid_mode):
    bsz, t, _ = u.shape
    if grid_mode:
        rows = t // GRID_W
        pad_rows = max(rows * (GRID_W + SEG_GAP) + SEG_GAP, t + 2 * CONV_PAD * GRID_W)
        body = functools.partial(_grid_conv_kernel, t=t)
    else:
        pad_rows = t + 2 * SEG_GAP
        body = functools.partial(_seq_conv_kernel, t=t)
    return pl.pallas_call(
        body,
        grid=(bsz, D_CONV // LANES),
        in_specs=[
            pl.BlockSpec((1, t, LANES), lambda b, j: (b, 0, j)),
            pl.BlockSpec((4 * SUBLANES, LANES), lambda b, j: (0, j)),
        ],
        out_specs=pl.BlockSpec((1, t, LANES), lambda b, j: (b, 0, j)),
        out_shape=jax.ShapeDtypeStruct((bsz, t, D_CONV), BF16),
        scratch_shapes=[pltpu.VMEM((pad_rows, LANES), F32)],
        compiler_params=pltpu.CompilerParams(
            dimension_semantics=("parallel", "parallel"), vmem_limit_bytes=VMEM_LIMIT),
        name="grid_conv" if grid_mode else "seq_conv",
    )(u, w_pad)


def _delta_kernel(q_ref, k_ref, v_ref, gb_ref, s0_ref, o_ref, sfin_ref,
                  s_scr, u_scr, wq_scr, ak_scr, gl_scr, *, reverse, nblk, nb):
    i = pl.program_id(1)
    heads = range(DN_HEADS)
    nchunk = nb // CHUNK

    @pl.when(i == 0)
    def _():
        s_scr[...] = s0_ref[0]
        u_scr[...] = jnp.zeros(u_scr.shape, F32)
        wq_scr[...] = jnp.zeros(wq_scr.shape, BF16)
        ak_scr[...] = jnp.zeros(ak_scr.shape, BF16)
        gl_scr[...] = jnp.ones(gl_scr.shape, F32)

    def recurrence():
        s = [s_scr[h] for h in heads]
        for step in range(nchunk):
            c = (nchunk - 1 - step) if reverse else step
            r = [_dot(wq_scr[h, c], s[h].astype(BF16)) for h in heads]
            yield
            v_new = [(u_scr[h, pl.ds(c * CHUNK, CHUNK), :] - r[h][:CHUNK]).astype(BF16) for h in heads]
            av = [_dot(ak_scr[h, c], v_new[h]) for h in heads]
            for h in heads:
                o_ref[0, pl.ds(c * CHUNK, CHUNK), h * LANES:(h + 1) * LANES] = (
                    av[h][:CHUNK] + r[h][CHUNK:]).astype(o_ref.dtype)
            s = [av[h][CHUNK:]
                 + s[h] * jnp.concatenate([gl_scr[h, c]] * (DN_HEAD_DIM // SUBLANES), axis=0)
                 for h in heads]
            yield
        for h in heads:
            s_scr[h] = s[h]
        while True:
            yield

    rec = recurrence()
    half_steps_per_stage = -(-(2 * nchunk + 1) // 10)

    def advance():
        for _ in range(half_steps_per_stage):
            next(rec)

    advance()

    sb = SOLVE_BLOCK
    units = [(h, hf) for h in heads for hf in range(nb // sb)]
    nu = range(len(units))
    lgs = [(DN_HEADS if reverse else 0) + h for h, _ in units]
    tok = [slice(hf * sb, (hf + 1) * sb) for _, hf in units]
    qbf = [q_ref[0, tok[u], h * LANES:(h + 1) * LANES] for u, (h, _) in enumerate(units)]
    kbf = [k_ref[0, tok[u], h * LANES:(h + 1) * LANES] for u, (h, _) in enumerate(units)]

    def chunk_masks(n_tok):
        row = lax.broadcasted_iota(jnp.int32, (n_tok, n_tok), 0)
        col = lax.broadcasted_iota(jnp.int32, (n_tok, n_tok), 1)
        same = (row // CHUNK) == (col // CHUNK)
        before = (col > row) if reverse else (col < row)
        return row, col, same, before

    _, _, same_nb, before_nb = chunk_masks(nb)
    gbt = gb_ref[0]
    nrow = 4 * DN_HEADS
    cum_mask = jnp.logical_and(same_nb, jnp.logical_not(before_nb))
    gc_t = _dot_mask_f32(gbt, jnp.where(cum_mask, 1.0, 0.0).astype(BF16))
    tot_t = _dot_mask_f32(gbt, jnp.where(same_nb, 1.0, 0.0).astype(BF16))
    cols = jnp.concatenate([gbt, gc_t, tot_t, jnp.zeros((LANES - 3 * nrow, nb), F32)], axis=0).T
    advance()

    row, col, same, before = chunk_masks(sb)
    strict = jnp.logical_and(same, before)
    incl = jnp.logical_or(strict, row == col)
    base_blocks = (row // INV_BASE) == (col // INV_BASE)

    q = [qbf[u].astype(F32) for u in nu]
    k = [kbf[u].astype(F32) for u in nu]
    v = [v_ref[0, tok[u], h * LANES:(h + 1) * LANES].astype(F32) for u, (h, _) in enumerate(units)]
    beta = [cols[tok[u], 2 * DN_HEADS + lgs[u]:2 * DN_HEADS + lgs[u] + 1] for u in nu]
    gcol = [cols[tok[u], nrow + lgs[u]:nrow + lgs[u] + 1] for u in nu]
    tcol = [cols[tok[u], 2 * nrow + lgs[u]:2 * nrow + lgs[u] + 1] for u in nu]
    decay = [jnp.where(incl, jnp.exp(jnp.where(incl, gcol[u] - gc_t[lgs[u]:lgs[u] + 1, tok[u]], 0.0)), 0.0)
             for u in nu]
    kb = [k[u] * beta[u] for u in nu]
    eg = [jnp.exp(gcol[u]) for u in nu]
    lfull = [_dot_nt(kb[u].astype(BF16), kbf[u]) * jnp.where(strict, decay[u], 0.0) for u in nu]
    advance()
    l0 = [jnp.where(base_blocks, lfull[u], 0.0) for u in nu]
    n = [-l0[u] for u in nu]
    pb = [l0[u].astype(BF16) for u in nu]
    p = [_dot(pb[u], pb[u]) for u in nu]
    advance()
    for it in range(3):
        pb = [p[u].astype(BF16) for u in nu]
        n = [_dot(n[u].astype(BF16), pb[u]) + (n[u] + p[u]) for u in nu]
        advance()
        if it < 2:
            p = [_dot(pb[u], pb[u]) for u in nu]
            advance()
    size = INV_BASE
    while size < CHUNK:
        picked = [j for j in range(sb // size) if j % 2 == (0 if reverse else 1)]

        def take(m):
            return jnp.concatenate([m[j * size:(j + 1) * size] for j in picked], axis=0)

        def put(full_blocks, picked_rows):
            blocks = list(full_blocks)
            for idx, j in enumerate(picked):
                blocks[j] = picked_rows[idx * size:(idx + 1) * size]
            return jnp.concatenate(blocks, axis=0)

        prow = lax.broadcasted_iota(jnp.int32, (sb // 2, sb), 0)
        pcol = lax.broadcasted_iota(jnp.int32, (sb // 2, sb), 1)
        orig = (prow // size) * (2 * size) + (0 if reverse else size) + prow % size
        pair_mask = jnp.logical_and(orig // (2 * size) == pcol // (2 * size), orig // size != pcol // size)
        zero_blocks = [jnp.zeros((size, sb), BF16)] * (sb // size)
        cm = [jnp.where(pair_mask, take(lfull[u]), 0.0) for u in nu]
        nbf = [n[u].astype(BF16) for u in nu]
        xm = [_dot(cm[u].astype(BF16), nbf[u]) + cm[u] for u in nu]
        advance()
        ym = [_dot(take(nbf[u]), put(zero_blocks, xm[u].astype(BF16))) + xm[u] for u in nu]
        n = [put([n[u][j * size:(j + 1) * size] for j in range(sb // size)], take(n[u]) - ym[u])
             for u in nu]
        advance()
        size *= 2
    rhs = [jnp.concatenate([v[u] * beta[u], kb[u] * eg[u]], axis=-1) for u in nu]
    y = [_dot(n[u].astype(BF16), rhs[u].astype(BF16)) + rhs[u] for u in nu]
    advance()
    attn = [_dot_nt(qbf[u], kbf[u]) * decay[u] for u in nu]
    for _ in range(2 * nchunk + 1):
        next(rec)
    for u, (h, hf) in enumerate(units):
        kt_t = (k[u] * jnp.exp(tcol[u] - gcol[u])).T
        qg = q[u] * eg[u]
        u_scr[h, pl.ds(hf * sb, sb), :] = y[u][:, :LANES]
        for cc in range(sb // CHUNK):
            c = hf * (sb // CHUNK) + cc
            sl = slice(cc * CHUNK, (cc + 1) * CHUNK)
            wq_scr[h, c, 0:CHUNK, :] = y[u][sl, LANES:].astype(BF16)
            wq_scr[h, c, CHUNK:2 * CHUNK, :] = qg[sl, :].astype(BF16)
            ak_scr[h, c, 0:CHUNK, :] = attn[u][sl, sl].astype(BF16)
            ak_scr[h, c, CHUNK:, :] = kt_t[:, sl].astype(BF16)
            gl_scr[h, c] = jnp.broadcast_to(
                jnp.exp(tot_t[lgs[u]:lgs[u] + 1, c * CHUNK:c * CHUNK + 1]), (SUBLANES, LANES))

    @pl.when(i == nblk)
    def _():
        sfin_ref[0] = s_scr[...]


def _delta(q, k, v, gb, s0, *, reverse):
    bsz, t, _ = q.shape
    nb = min(DELTA_BLOCK, t)
    nblk = t // nb
    nchunk = nb // CHUNK
    if reverse:
        bidx = lambda i: nblk - 1 - i
    else:
        bidx = lambda i: i
    blk = lambda n: pl.BlockSpec((1, nb, n), lambda b, i: (b, bidx(jnp.minimum(i, nblk - 1)), 0))
    gb_blk = pl.BlockSpec((1, 4 * DN_HEADS, nb), lambda b, i: (b, 0, bidx(jnp.minimum(i, nblk - 1))))
    out_blk = pl.BlockSpec((1, nb, D_DN), lambda b, i: (b, bidx(jnp.maximum(i - 1, 0)), 0))
    state_spec = pl.BlockSpec((1, DN_HEADS, DN_HEAD_DIM, DN_HEAD_DIM), lambda b, i: (b, 0, 0, 0))
    return pl.pallas_call(
        functools.partial(_delta_kernel, reverse=reverse, nblk=nblk, nb=nb),
        grid=(bsz, nblk + 1),
        in_specs=[blk(D_DN), blk(D_DN), blk(D_DN), gb_blk, state_spec],
        out_specs=[out_blk, state_spec],
        out_shape=[
            jax.ShapeDtypeStruct((bsz, t, D_DN), BF16),
            jax.ShapeDtypeStruct((bsz, DN_HEADS, DN_HEAD_DIM, DN_HEAD_DIM), F32),
        ],
        scratch_shapes=[
            pltpu.VMEM((DN_HEADS, DN_HEAD_DIM, DN_HEAD_DIM), F32),
            pltpu.VMEM((DN_HEADS, nb, LANES), F32),
            pltpu.VMEM((DN_HEADS, nchunk, 2 * CHUNK, LANES), BF16),
            pltpu.VMEM((DN_HEADS, nchunk, CHUNK + DN_HEAD_DIM, CHUNK), BF16),
            pltpu.VMEM((DN_HEADS, nchunk, SUBLANES, LANES), F32),
        ],
        compiler_params=pltpu.CompilerParams(
            dimension_semantics=("arbitrary", "arbitrary"), vmem_limit_bytes=VMEM_LIMIT),
        name="delta_bwd" if reverse else "delta_fwd",
    )(q, k, v, gb, s0)


def _outproj_kernel(*refs, final):
    if final:
        (conv_ref, cgs_ref, of_ref, ob_ref, zs_ref, x_ref, cb_ref, lnw_ref, lnb_ref, dnw_ref,
         gate_ref, wc_ref, wd_ref, fnw_ref, o_ref) = refs
    else:
        (conv_ref, cgs_ref, of_ref, ob_ref, zs_ref, x_ref, cb_ref, lnw_ref, lnb_ref, dnw_ref,
         gate_ref, wc_ref, wd_ref, o_ref) = refs
    yc = conv_ref[0].astype(F32) + cb_ref[...]
    mu = jnp.mean(yc, axis=-1, keepdims=True)
    d = yc - mu
    var = jnp.mean(d * d, axis=-1, keepdims=True)
    yn = d * lax.rsqrt(var + LN_EPS) * lnw_ref[...] + lnb_ref[...]
    y_conv = (_silu(yn) * cgs_ref[0].astype(F32)).astype(BF16)
    o = of_ref[0].astype(F32) + ob_ref[0].astype(F32)
    acc = _dot(y_conv, wc_ref[...])
    for h in range(DN_HEADS):
        sl = slice(h * LANES, (h + 1) * LANES)
        oh = o[:, sl]
        on = oh * lax.rsqrt(jnp.mean(oh * oh, axis=-1, keepdims=True) + EPS) * dnw_ref[...]
        y_dn = (on * zs_ref[0, :, sl].astype(F32)).astype(BF16)
        acc = acc + _dot(y_dn, wd_ref[sl, :])
    xn = x_ref[0] + gate_ref[0] * acc
    if final:
        xn = xn * lax.rsqrt(jnp.mean(xn * xn, axis=-1, keepdims=True) + EPS) * fnw_ref[...]
    o_ref[0] = xn


def _outproj(conv, cgs, o_f, o_b, zs, x, cb, lnw, lnb, dnw, gate, wc, wd, fnw, *, tm):
    bsz, t, _ = x.shape
    final = fnw is not None
    row = lambda n: pl.BlockSpec((1, tm, n), lambda b, i: (b, i, 0))
    const = lambda shp: pl.BlockSpec(shp, lambda b, i: (0,) * len(shp))
    per_b = pl.BlockSpec((1, 1, D_MODEL), lambda b, i: (b, 0, 0))
    in_specs = [row(D_CONV), row(D_CONV), row(D_DN), row(D_DN), row(D_DN), row(D_MODEL),
                const((1, D_CONV)), const((1, D_CONV)), const((1, D_CONV)), const((1, DN_HEAD_DIM)),
                per_b, const((D_CONV, D_MODEL)), const((D_DN, D_MODEL))]
    args = [conv, cgs, o_f, o_b, zs, x, cb, lnw, lnb, dnw, gate, wc, wd]
    if final:
        in_specs.append(const((1, D_MODEL)))
        args.append(fnw)
    return pl.pallas_call(
        functools.partial(_outproj_kernel, final=final),
        grid=(bsz, t // tm),
        in_specs=in_specs,
        out_specs=row(D_MODEL),
        out_shape=jax.ShapeDtypeStruct((bsz, t, D_MODEL), F32),
        compiler_params=pltpu.CompilerParams(
            dimension_semantics=("parallel", "parallel"), vmem_limit_bytes=VMEM_LIMIT),
        name="outproj_final" if final else "outproj",
    )(*args)


def _layer_weights(l, norm_w, w_in, conv_w, conv_b, conv_ln_w, conv_ln_b, short_conv_w, a_log,
                   dt_bias, dn_norm_w, w_out):
    wi = w_in[l]
    c0 = 3 * D_CONV
    w_ab = jnp.zeros((D_MODEL, LANES), F32).at[:, :4 * DN_HEADS].set(wi[:, c0 + 4 * D_DN:])
    pad_vec = lambda v: v.reshape(2 * DN_HEADS, 1)
    return dict(
        nw=norm_w[l].reshape(1, D_MODEL),
        w_full=[wi[:, :D_CONV].astype(BF16), wi[:, D_CONV:2 * D_CONV].astype(BF16),
                wi[:, 2 * D_CONV:c0].astype(BF16), wi[:, c0 + 3 * D_DN:c0 + 4 * D_DN].astype(BF16)],
        w_dn=[wi[:, c0:c0 + 3 * D_DN].astype(BF16), w_ab.astype(BF16)],
        alog=pad_vec(a_log[l]), dtb=pad_vec(dt_bias[l]),
        conv_w=jnp.zeros((4 * SUBLANES, D_CONV), F32).at[:CONV_W].set(conv_w[l]),
        conv_b=conv_b[l].reshape(1, D_CONV), lnw=conv_ln_w[l].reshape(1, D_CONV),
        lnb=conv_ln_b[l].reshape(1, D_CONV),
        scw=jnp.zeros((SUBLANES, 3 * D_DN), F32).at[:SHORT_CONV_W].set(short_conv_w[l]),
        dnw=dn_norm_w[l].reshape(1, DN_HEAD_DIM),
        wc=w_out[l, :D_CONV].astype(BF16), wd=w_out[l, D_CONV:].astype(BF16),
    )


def _mixer(x, mod, p, s0_f, s0_b, *, grid_mode, tm, fnw=None, states_only=False):
    bsz = x.shape[0]
    shift, scale, gate = (mod[:, j * D_MODEL:(j + 1) * D_MODEL].reshape(bsz, 1, D_MODEL) for j in range(3))
    if states_only:
        q, k, v, gb = _inproj(x, p["nw"], shift, scale, p["w_dn"], p["alog"], p["dtb"], p["scw"],
                              full=False, tm=tm)
        _, s_f = _delta(q, k, v, gb, s0_f, reverse=False)
        _, s_b = _delta(q, k, v, gb, s0_b, reverse=True)
        return x, s_f, s_b
    u, cgs, zs, q, k, v, gb = _inproj(x, p["nw"], shift, scale, p["w_full"] + p["w_dn"], p["alog"],
                                      p["dtb"], p["scw"], full=True, tm=tm)
    conv = _dwconv(u, p["conv_w"], grid_mode=grid_mode)
    o_f, s_f = _delta(q, k, v, gb, s0_f, reverse=False)
    o_b, s_b = _delta(q, k, v, gb, s0_b, reverse=True)
    x_new = _outproj(conv, cgs, o_f, o_b, zs, x, p["conv_b"], p["lnw"], p["lnb"], p["dnw"], gate,
                     p["wc"], p["wd"], fnw, tm=min(x.shape[1], OUT_TILE))
    return x_new, s_f, s_b


def kernel(x, c, ctx, c_ctx, norm_w, w_ada, b_ada, w_in, conv_w, conv_b, conv_ln_w, conv_ln_b, short_conv_w, a_log, dt_bias, dn_norm_w, w_out, final_norm_w):
    bsz = x.shape[0]
    cond_t = jnp.zeros((D_MODEL, LANES), F32).at[:, :bsz].set(c.T).at[:, bsz].set(c_ctx)
    mods = _ada_all(cond_t, w_ada, b_ada, nrows=bsz + 1)
    zeros = jnp.zeros((bsz, DN_HEADS, DN_HEAD_DIM, DN_HEAD_DIM), F32)
    for l in range(DEPTH):
        p = _layer_weights(l, norm_w, w_in, conv_w, conv_b, conv_ln_w, conv_ln_b, short_conv_w,
                           a_log, dt_bias, dn_norm_w, w_out)
        mod_ctx = jnp.broadcast_to(mods[l, bsz:bsz + 1], (bsz, 3 * D_MODEL))
        last = l == DEPTH - 1
        ctx_new, s_f, s_b = _mixer(ctx, mod_ctx, p, zeros, zeros, grid_mode=False, tm=ctx.shape[1],
                                   states_only=last)
        x, _, _ = _mixer(x, mods[l, :bsz], p, s_f, s_b, grid_mode=True, tm=512,
                         fnw=final_norm_w.reshape(1, D_MODEL) if last else None)
        ctx = ctx_new
    return x
```

```python
import functools

import jax
import jax.numpy as jnp
from jax import lax
from jax.experimental import pallas as pl
from jax.experimental.pallas import tpu as pltpu

D_MODEL = 1024
DEPTH = 4
GRID_W = 64
D_CONV = 512
CONV_W = 31
CONV_PAD = (CONV_W - 1) // 2
DN_HEADS = 4
DN_HEAD_DIM = 128
D_DN = DN_HEADS * DN_HEAD_DIM
SHORT_CONV_W = 5
SHORT_PAD = (SHORT_CONV_W - 1) // 2
CHUNK = 64
EPS = 1e-6
LN_EPS = 1e-5

LANES = 128
SUBLANES = 8
IN_TILE = 1024
OUT_TILE = 1024
DELTA_BLOCK = 512
SOLVE_BLOCK = 128
INV_BASE = 16
BF16_ROWS = 16
HALO = BF16_ROWS
CONV_UNROLL = 4
SEG_GAP = 16
VMEM_LIMIT = 48 * 1024 * 1024

F32 = jnp.float32
BF16 = jnp.bfloat16


def _dot(a, b):
    return jnp.dot(a, b, preferred_element_type=F32)


def _dot_nt(a, b):
    return lax.dot_general(a, b, (((1,), (1,)), ((), ())), preferred_element_type=F32)


def _sigmoid(x):
    return 0.5 * jnp.tanh(0.5 * x) + 0.5


def _silu(x):
    half = 0.5 * x
    return half * jnp.tanh(half) + half


def _softplus(x):
    return jnp.maximum(x, 0.0) + jnp.log(1.0 + jnp.exp(-jnp.abs(x)))


def _dot_mask_f32(x, mask_bf16):
    hi = x.astype(BF16)
    r1 = x - hi.astype(F32)
    mid = r1.astype(BF16)
    lo = (r1 - mid.astype(F32)).astype(BF16)
    return _dot(hi, mask_bf16) + _dot(mid, mask_bf16) + _dot(lo, mask_bf16)


def _ada_kernel(cond_t_ref, w_ref, b_ref, o_ref, *, nrows):
    s_t = _silu(cond_t_ref[...])
    w = w_ref[0]
    rows = [jnp.sum(s_t[:, r:r + 1] * w, axis=0, keepdims=True) for r in range(nrows)]
    rows.append(jnp.zeros((SUBLANES - nrows, w.shape[1]), F32))
    o_ref[0] = jnp.concatenate(rows, axis=0) + b_ref[0]


def _ada_all(cond_t, w_ada, b_ada, *, nrows):
    tn = 3 * D_MODEL
    return pl.pallas_call(
        functools.partial(_ada_kernel, nrows=nrows),
        grid=(DEPTH, 3 * D_MODEL // tn),
        in_specs=[
            pl.BlockSpec((D_MODEL, LANES), lambda l, j: (0, 0)),
            pl.BlockSpec((1, D_MODEL, tn), lambda l, j: (l, 0, j)),
            pl.BlockSpec((1, 1, tn), lambda l, j: (l, 0, j)),
        ],
        out_specs=pl.BlockSpec((1, SUBLANES, tn), lambda l, j: (l, 0, j)),
        out_shape=jax.ShapeDtypeStruct((DEPTH, SUBLANES, 3 * D_MODEL), F32),
        compiler_params=pltpu.CompilerParams(vmem_limit_bytes=VMEM_LIMIT),
        name="ada",
    )(cond_t, w_ada, b_ada.reshape(DEPTH, 1, 3 * D_MODEL))


def _inproj_kernel(*refs, full, tm, nblk):
    if full:
        (prev_ref, x_ref, next_ref, nw_ref, shift_ref, scale_ref, wa_ref, wb_ref, wcg_ref, wz_ref,
         wqkv_ref, wab_ref, alog_ref, dtb_ref, scw_ref,
         u_ref, cgs_ref, zs_ref, q_ref, k_ref, v_ref, gb_ref, h_scr, qkv_scr) = refs
    else:
        (prev_ref, x_ref, next_ref, nw_ref, shift_ref, scale_ref, wqkv_ref, wab_ref, alog_ref, dtb_ref,
         scw_ref, q_ref, k_ref, v_ref, gb_ref, h_scr, qkv_scr) = refs
    i = pl.program_id(1)
    gain = nw_ref[...] * (1.0 + scale_ref[0])

    def modulated(x):
        y = x * lax.rsqrt(jnp.mean(x * x, axis=-1, keepdims=True) + EPS)
        return (y * gain + shift_ref[0]).astype(BF16)

    h_scr[pl.ds(0, HALO), :] = modulated(prev_ref[0])
    h_scr[pl.ds(HALO, tm), :] = modulated(x_ref[0])
    h_scr[pl.ds(HALO + tm, HALO), :] = modulated(next_ref[0])

    def h_dot(w_ref):
        return _dot(h_scr[pl.ds(HALO, tm), :], w_ref[...])

    qkv = _dot(h_scr[...], wqkv_ref[...])
    for grp in range(3 * DN_HEADS):
        cols = slice(grp * LANES, (grp + 1) * LANES)
        qkv_scr[grp, pl.ds(0, HALO), :] = jnp.where(i > 0, qkv[:HALO, cols], 0.0)
        qkv_scr[grp, pl.ds(HALO, tm), :] = qkv[HALO:HALO + tm, cols]
        qkv_scr[grp, pl.ds(HALO + tm, HALO), :] = jnp.where(i < nblk - 1, qkv[HALO + tm:, cols], 0.0)

    def conv_silu(grp):
        acc = jnp.zeros((tm, LANES), F32)
        for j in range(SHORT_CONV_W):
            acc = acc + (qkv_scr[grp, pl.ds(HALO - SHORT_PAD + j, tm), :]
                         * scw_ref[j:j + 1, grp * LANES:(grp + 1) * LANES])
        return _silu(acc)

    def l2n(m):
        return m * lax.rsqrt(jnp.sum(m * m, axis=-1, keepdims=True) + EPS)

    def qkv_epilogue(hd):
        sl = slice(hd * LANES, (hd + 1) * LANES)
        q_ref[0, :, sl] = (l2n(conv_silu(hd)) * (DN_HEAD_DIM ** -0.5)).astype(q_ref.dtype)
        k_ref[0, :, sl] = l2n(conv_silu(DN_HEADS + hd)).astype(k_ref.dtype)
        v_ref[0, :, sl] = conv_silu(2 * DN_HEADS + hd).astype(v_ref.dtype)

    if full:
        u_ref[0] = h_dot(wa_ref) * _sigmoid(h_dot(wb_ref))
        qkv_epilogue(0)
        cgs_ref[0] = _silu(h_dot(wcg_ref)).astype(cgs_ref.dtype)
        qkv_epilogue(1)
        zs_ref[0] = _silu(h_dot(wz_ref)).astype(zs_ref.dtype)
        qkv_epilogue(2)
    else:
        for hd in range(DN_HEADS - 1):
            qkv_epilogue(hd)
    ab_t = h_dot(wab_ref).T
    g_t = -jnp.exp(alog_ref[...]) * _softplus(ab_t[0:2 * DN_HEADS] + dtb_ref[...])
    gb_ref[0] = jnp.concatenate([g_t, _sigmoid(ab_t[2 * DN_HEADS:4 * DN_HEADS])], axis=0)
    qkv_epilogue(DN_HEADS - 1)


def _inproj(x, nw, shift, scale, wts, alog, dtb, scw, *, full, tm):
    bsz, t, _ = x.shape
    nblk = t // tm
    halo_per_blk = tm // HALO
    last_halo = t // HALO - 1
    row = lambda n: pl.BlockSpec((1, tm, n), lambda b, i: (b, i, 0))
    const = lambda shp: pl.BlockSpec(shp, lambda b, i: (0,) * len(shp))
    per_b = pl.BlockSpec((1, 1, D_MODEL), lambda b, i: (b, 0, 0))
    halo_prev = pl.BlockSpec((1, HALO, D_MODEL),
                             lambda b, i: (b, jnp.maximum(i * halo_per_blk - 1, 0), 0))
    halo_next = pl.BlockSpec((1, HALO, D_MODEL),
                             lambda b, i: (b, jnp.minimum((i + 1) * halo_per_blk, last_halo), 0))
    w_specs = [pl.BlockSpec(w.shape, lambda b, i: (0, 0), pipeline_mode=pl.Buffered(1)) for w in wts]
    in_specs = ([halo_prev, row(D_MODEL), halo_next, const((1, D_MODEL)), per_b, per_b] + w_specs
                + [const((2 * DN_HEADS, 1))] * 2 + [const((SUBLANES, 3 * D_DN))])
    outs = ([(D_CONV, F32), (D_CONV, BF16), (D_DN, BF16)] if full else []) + [
        (D_DN, BF16), (D_DN, BF16), (D_DN, BF16)]
    gb_spec = pl.BlockSpec((1, 4 * DN_HEADS, tm), lambda b, i: (b, 0, i))
    gb_shape = jax.ShapeDtypeStruct((bsz, 4 * DN_HEADS, t), F32)
    return pl.pallas_call(
        functools.partial(_inproj_kernel, full=full, tm=tm, nblk=nblk),
        grid=(bsz, nblk),
        in_specs=in_specs,
        out_specs=[row(n) for n, _ in outs] + [gb_spec],
        out_shape=[jax.ShapeDtypeStruct((bsz, t, n), dt) for n, dt in outs] + [gb_shape],
        scratch_shapes=[pltpu.VMEM((tm + 2 * HALO, D_MODEL), BF16),
                        pltpu.VMEM((3 * DN_HEADS, tm + 2 * HALO, LANES), F32)],
        compiler_params=pltpu.CompilerParams(
            dimension_semantics=("parallel", "parallel"), vmem_limit_bytes=VMEM_LIMIT),
        name="inproj_full" if full else "inproj_dn",
    )(x, x, x, nw, shift, scale, *wts, alog, dtb, scw)


def _conv_segments(u_ref, w_ref, o_ref, pad_ref, *, seg, nseg):
    stride = seg + SEG_GAP
    zeros_gap = jnp.zeros((SEG_GAP, LANES), F32)

    def fill(r, carry):
        base = pl.multiple_of(r * stride, SUBLANES)
        src = pl.multiple_of(r * seg, SUBLANES)
        pad_ref[pl.ds(base, SEG_GAP), :] = zeros_gap
        pad_ref[pl.ds(base + SEG_GAP, seg), :] = u_ref[0, pl.ds(src, seg), :]
        return carry

    lax.fori_loop(0, nseg, fill, 0)
    pad_ref[pl.ds(nseg * stride, SEG_GAP), :] = zeros_gap

    def body(r, carry):
        base = pl.multiple_of(r * stride, SUBLANES)
        dst = pl.multiple_of(r * seg, BF16_ROWS)
        acc = jnp.zeros((seg, LANES), F32)
        for k in range(CONV_W):
            acc = acc + pad_ref[pl.ds(base + (SEG_GAP - CONV_PAD) + k, seg), :] * w_ref[k:k + 1, :]
        o_ref[0, pl.ds(dst, seg), :] = acc.astype(o_ref.dtype)
        return carry

    lax.fori_loop(0, nseg, body, 0, unroll=CONV_UNROLL if nseg % CONV_UNROLL == 0 else 1)


def _conv_strided(u_ref, w_ref, o_ref, pad_ref, *, t, step):
    halo = CONV_PAD * step
    pad_ref[pl.ds(0, halo), :] = jnp.zeros((halo, LANES), F32)
    pad_ref[pl.ds(halo + t, halo), :] = jnp.zeros((halo, LANES), F32)
    pad_ref[pl.ds(halo, t), :] = u_ref[0]

    def body(r, carry):
        base = pl.multiple_of(r * step, BF16_ROWS)
        acc = jnp.zeros((step, LANES), F32)
        for k in range(CONV_W):
            acc = acc + pad_ref[pl.ds(base + k * step, step), :] * w_ref[k:k + 1, :]
        o_ref[0, pl.ds(base, step), :] = acc.astype(o_ref.dtype)
        return carry

    lax.fori_loop(0, t // step, body, 0, unroll=CONV_UNROLL if (t // step) % CONV_UNROLL == 0 else 1)


def _grid_conv_kernel(u_ref, w_ref, o_ref, pad_ref, *, t):
    j = pl.program_id(1)
    half_blocks = D_CONV // LANES // 2

    @pl.when(j < half_blocks)
    def _():
        _conv_segments(u_ref, w_ref, o_ref, pad_ref, seg=GRID_W, nseg=t // GRID_W)

    @pl.when(j >= half_blocks)
    def _():
        _conv_strided(u_ref, w_ref, o_ref, pad_ref, t=t, step=GRID_W)


def _seq_conv_kernel(u_ref, w_ref, o_ref, pad_ref, *, t):
    _conv_segments(u_ref, w_ref, o_ref, pad_ref, seg=t, nseg=1)


def _dwconv(u, w_pad, *, grid_mode):
    bsz, t, _ = u.shape
    if grid_mode:
        rows = t // GRID_W
        pad_rows = max(rows * (GRID_W + SEG_GAP) + SEG_GAP, t + 2 * CONV_PAD * GRID_W)
        body = functools.partial(_grid_conv_kernel, t=t)
    else:
        pad_rows = t + 2 * SEG_GAP
        body = functools.partial(_seq_conv_kernel, t=t)
    return pl.pallas_call(
        body,
        grid=(bsz, D_CONV // LANES),
        in_specs=[
            pl.BlockSpec((1, t, LANES), lambda b, j: (b, 0, j)),
            pl.BlockSpec((4 * SUBLANES, LANES), lambda b, j: (0, j)),
        ],
        out_specs=pl.BlockSpec((1, t, LANES), lambda b, j: (b, 0, j)),
        out_shape=jax.ShapeDtypeStruct((bsz, t, D_CONV), BF16),
        scratch_shapes=[pltpu.VMEM((pad_rows, LANES), F32)],
        compiler_params=pltpu.CompilerParams(
            dimension_semantics=("parallel", "parallel"), vmem_limit_bytes=VMEM_LIMIT),
        name="grid_conv" if grid_mode else "seq_conv",
    )(u, w_pad)


def _delta_kernel(q_ref, k_ref, v_ref, gb_ref, s0_ref, o_ref, sfin_ref,
                  s_scr, u_scr, wq_scr, ak_scr, gl_scr, *, reverse, nblk, nb):
    i = pl.program_id(1)
    heads = range(DN_HEADS)
    nchunk = nb // CHUNK

    @pl.when(i == 0)
    def _():
        s_scr[...] = s0_ref[0]
        u_scr[...] = jnp.zeros(u_scr.shape, F32)
        wq_scr[...] = jnp.zeros(wq_scr.shape, BF16)
        ak_scr[...] = jnp.zeros(ak_scr.shape, BF16)
        gl_scr[...] = jnp.ones(gl_scr.shape, F32)

    def recurrence():
        s = [s_scr[h] for h in heads]
        for step in range(nchunk):
            c = (nchunk - 1 - step) if reverse else step
            r = [_dot(wq_scr[h, c], s[h].astype(BF16)) for h in heads]
            yield
            v_new = [(u_scr[h, pl.ds(c * CHUNK, CHUNK), :] - r[h][:CHUNK]).astype(BF16) for h in heads]
            av = [_dot(ak_scr[h, c], v_new[h]) for h in heads]
            for h in heads:
                o_ref[0, pl.ds(c * CHUNK, CHUNK), h * LANES:(h + 1) * LANES] = (
                    av[h][:CHUNK] + r[h][CHUNK:]).astype(o_ref.dtype)
            s = [av[h][CHUNK:]
                 + s[h] * jnp.concatenate([gl_scr[h, c]] * (DN_HEAD_DIM // SUBLANES), axis=0)
                 for h in heads]
            yield
        for h in heads:
            s_scr[h] = s[h]
        while True:
            yield

    rec = recurrence()
    half_steps_per_stage = -(-(2 * nchunk + 1) // 10)

    def advance():
        for _ in range(half_steps_per_stage):
            next(rec)

    advance()

    sb = SOLVE_BLOCK
    units = [(h, hf) for h in heads for hf in range(nb // sb)]
    nu = range(len(units))
    lgs = [(DN_HEADS if reverse else 0) + h for h, _ in units]
    tok = [slice(hf * sb, (hf + 1) * sb) for _, hf in units]
    qbf = [q_ref[0, tok[u], h * LANES:(h + 1) * LANES] for u, (h, _) in enumerate(units)]
    kbf = [k_ref[0, tok[u], h * LANES:(h + 1) * LANES] for u, (h, _) in enumerate(units)]

    def chunk_masks(n_tok):
        row = lax.broadcasted_iota(jnp.int32, (n_tok, n_tok), 0)
        col = lax.broadcasted_iota(jnp.int32, (n_tok, n_tok), 1)
        same = (row // CHUNK) == (col // CHUNK)
        before = (col > row) if reverse else (col < row)
        return row, col, same, before

    _, _, same_nb, before_nb = chunk_masks(nb)
    gbt = gb_ref[0]
    nrow = 4 * DN_HEADS
    cum_mask = jnp.logical_and(same_nb, jnp.logical_not(before_nb))
    gc_t = _dot_mask_f32(gbt, jnp.where(cum_mask, 1.0, 0.0).astype(BF16))
    tot_t = _dot_mask_f32(gbt, jnp.where(same_nb, 1.0, 0.0).astype(BF16))
    cols = jnp.concatenate([gbt, gc_t, tot_t, jnp.zeros((LANES - 3 * nrow, nb), F32)], axis=0).T
    advance()

    row, col, same, before = chunk_masks(sb)
    strict = jnp.logical_and(same, before)
    incl = jnp.logical_or(strict, row == col)
    base_blocks = (row // INV_BASE) == (col // INV_BASE)

    q = [qbf[u].astype(F32) for u in nu]
    k = [kbf[u].astype(F32) for u in nu]
    v = [v_ref[0, tok[u], h * LANES:(h + 1) * LANES].astype(F32) for u, (h, _) in enumerate(units)]
    beta = [cols[tok[u], 2 * DN_HEADS + lgs[u]:2 * DN_HEADS + lgs[u] + 1] for u in nu]
    gcol = [cols[tok[u], nrow + lgs[u]:nrow + lgs[u] + 1] for u in nu]
    tcol = [cols[tok[u], 2 * nrow + lgs[u]:2 * nrow + lgs[u] + 1] for u in nu]
    decay = [jnp.where(incl, jnp.exp(jnp.where(incl, gcol[u] - gc_t[lgs[u]:lgs[u] + 1, tok[u]], 0.0)), 0.0)
             for u in nu]
    kb = [k[u] * beta[u] for u in nu]
    eg = [jnp.exp(gcol[u]) for u in nu]
    lfull = [_dot_nt(kb[u].astype(BF16), kbf[u]) * jnp.where(strict, decay[u], 0.0) for u in nu]
    advance()
    l0 = [jnp.where(base_blocks, lfull[u], 0.0) for u in nu]
    n = [-l0[u] for u in nu]
    pb = [l0[u].astype(BF16) for u in nu]
    p = [_dot(pb[u], pb[u]) for u in nu]
    advance()
    for it in range(3):
        pb = [p[u].astype(BF16) for u in nu]
        n = [_dot(n[u].astype(BF16), pb[u]) + (n[u] + p[u]) for u in nu]
        advance()
        if it < 2:
            p = [_dot(pb[u], pb[u]) for u in nu]
            advance()
    size = INV_BASE
    while size < CHUNK:
        picked = [j for j in range(sb // size) if j % 2 == (0 if reverse else 1)]

        def take(m):
            return jnp.concatenate([m[j * size:(j + 1) * size] for j in picked], axis=0)

        def put(full_blocks, picked_rows):
            blocks = list(full_blocks)
            for idx, j in enumerate(picked):
                blocks[j] = picked_rows[idx * size:(idx + 1) * size]
            return jnp.concatenate(blocks, axis=0)

        prow = lax.broadcasted_iota(jnp.int32, (sb // 2, sb), 0)
        pcol = lax.broadcasted_iota(jnp.int32, (sb // 2, sb), 1)
        orig = (prow // size) * (2 * size) + (0 if reverse else size) + prow % size
        pair_mask = jnp.logical_and(orig // (2 * size) == pcol // (2 * size), orig // size != pcol // size)
        zero_blocks = [jnp.zeros((size, sb), BF16)] * (sb // size)
        cm = [jnp.where(pair_mask, take(lfull[u]), 0.0) for u in nu]
        nbf = [n[u].astype(BF16) for u in nu]
        xm = [_dot(cm[u].astype(BF16), nbf[u]) + cm[u] for u in nu]
        advance()
        ym = [_dot(take(nbf[u]), put(zero_blocks, xm[u].astype(BF16))) + xm[u] for u in nu]
        n = [put([n[u][j * size:(j + 1) * size] for j in range(sb // size)], take(n[u]) - ym[u])
             for u in nu]
        advance()
        size *= 2
    rhs = [jnp.concatenate([v[u] * beta[u], kb[u] * eg[u]], axis=-1) for u in nu]
    y = [_dot(n[u].astype(BF16), rhs[u].astype(BF16)) + rhs[u] for u in nu]
    advance()
    attn = [_dot_nt(qbf[u], kbf[u]) * decay[u] for u in nu]
    for _ in range(2 * nchunk + 1):
        next(rec)
    for u, (h, hf) in enumerate(units):
        kt_t = (k[u] * jnp.exp(tcol[u] - gcol[u])).T
        qg = q[u] * eg[u]
        u_scr[h, pl.ds(hf * sb, sb), :] = y[u][:, :LANES]
        for cc in range(sb // CHUNK):
            c = hf * (sb // CHUNK) + cc
            sl = slice(cc * CHUNK, (cc + 1) * CHUNK)
            wq_scr[h, c, 0:CHUNK, :] = y[u][sl, LANES:].astype(BF16)
            wq_scr[h, c, CHUNK:2 * CHUNK, :] = qg[sl, :].astype(BF16)
            ak_scr[h, c, 0:CHUNK, :] = attn[u][sl, sl].astype(BF16)
            ak_scr[h, c, CHUNK:, :] = kt_t[:, sl].astype(BF16)
            gl_scr[h, c] = jnp.broadcast_to(
                jnp.exp(tot_t[lgs[u]:lgs[u] + 1, c * CHUNK:c * CHUNK + 1]), (SUBLANES, LANES))

    @pl.when(i == nblk)
    def _():
        sfin_ref[0] = s_scr[...]


def _delta(q, k, v, gb, s0, *, reverse):
    bsz, t, _ = q.shape
    nb = min(DELTA_BLOCK, t)
    nblk = t // nb
    nchunk = nb // CHUNK
    if reverse:
        bidx = lambda i: nblk - 1 - i
    else:
        bidx = lambda i: i
    blk = lambda n: pl.BlockSpec((1, nb, n), lambda b, i: (b, bidx(jnp.minimum(i, nblk - 1)), 0))
    gb_blk = pl.BlockSpec((1, 4 * DN_HEADS, nb), lambda b, i: (b, 0, bidx(jnp.minimum(i, nblk - 1))))
    out_blk = pl.BlockSpec((1, nb, D_DN), lambda b, i: (b, bidx(jnp.maximum(i - 1, 0)), 0))
    state_spec = pl.BlockSpec((1, DN_HEADS, DN_HEAD_DIM, DN_HEAD_DIM), lambda b, i: (b, 0, 0, 0))
    return pl.pallas_call(
        functools.partial(_delta_kernel, reverse=reverse, nblk=nblk, nb=nb),
        grid=(bsz, nblk + 1),
        in_specs=[blk(D_DN), blk(D_DN), blk(D_DN), gb_blk, state_spec],
        out_specs=[out_blk, state_spec],
        out_shape=[
            jax.ShapeDtypeStruct((bsz, t, D_DN), BF16),
            jax.ShapeDtypeStruct((bsz, DN_HEADS, DN_HEAD_DIM, DN_HEAD_DIM), F32),
        ],
        scratch_shapes=[
            pltpu.VMEM((DN_HEADS, DN_HEAD_DIM, DN_HEAD_DIM), F32),
            pltpu.VMEM((DN_HEADS, nb, LANES), F32),
            pltpu.VMEM((DN_HEADS, nchunk, 2 * CHUNK, LANES), BF16),
            pltpu.VMEM((DN_HEADS, nchunk, CHUNK + DN_HEAD_DIM, CHUNK), BF16),
            pltpu.VMEM((DN_HEADS, nchunk, SUBLANES, LANES), F32),
        ],
        compiler_params=pltpu.CompilerParams(
            dimension_semantics=("arbitrary", "arbitrary"), vmem_limit_bytes=VMEM_LIMIT),
        name="delta_bwd" if reverse else "delta_fwd",
    )(q, k, v, gb, s0)


def _outproj_kernel(*refs, final):
    if final:
        (conv_ref, cgs_ref, of_ref, ob_ref, zs_ref, x_ref, cb_ref, lnw_ref, lnb_ref, dnw_ref,
         gate_ref, wc_ref, wd_ref, fnw_ref, o_ref) = refs
    else:
        (conv_ref, cgs_ref, of_ref, ob_ref, zs_ref, x_ref, cb_ref, lnw_ref, lnb_ref, dnw_ref,
         gate_ref, wc_ref, wd_ref, o_ref) = refs
    yc = conv_ref[0].astype(F32) + cb_ref[...]
    mu = jnp.mean(yc, axis=-1, keepdims=True)
    d = yc - mu
    var = jnp.mean(d * d, axis=-1, keepdims=True)
    yn = d * lax.rsqrt(var + LN_EPS) * lnw_ref[...] + lnb_ref[...]
    y_conv = (_silu(yn) * cgs_ref[0].astype(F32)).astype(BF16)
    o = of_ref[0].astype(F32) + ob_ref[0].astype(F32)
    acc = _dot(y_conv, wc_ref[...])
    for h in range(DN_HEADS):
        sl = slice(h * LANES, (h + 1) * LANES)
        oh = o[:, sl]
        on = oh * lax.rsqrt(jnp.mean(oh * oh, axis=-1, keepdims=True) + EPS) * dnw_ref[...]
        y_dn = (on * zs_ref[0, :, sl].astype(F32)).astype(BF16)
        acc = acc + _dot(y_dn, wd_ref[sl, :])
    xn = x_ref[0] + gate_ref[0] * acc
    if final:
        xn = xn * lax.rsqrt(jnp.mean(xn * xn, axis=-1, keepdims=True) + EPS) * fnw_ref[...]
    o_ref[0] = xn


def _outproj(conv, cgs, o_f, o_b, zs, x, cb, lnw, lnb, dnw, gate, wc, wd, fnw, *, tm):
    bsz, t, _ = x.shape
    final = fnw is not None
    row = lambda n: pl.BlockSpec((1, tm, n), lambda b, i: (b, i, 0))
    const = lambda shp: pl.BlockSpec(shp, lambda b, i: (0,) * len(shp))
    per_b = pl.BlockSpec((1, 1, D_MODEL), lambda b, i: (b, 0, 0))
    in_specs = [row(D_CONV), row(D_CONV), row(D_DN), row(D_DN), row(D_DN), row(D_MODEL),
                const((1, D_CONV)), const((1, D_CONV)), const((1, D_CONV)), const((1, DN_HEAD_DIM)),
                per_b, const((D_CONV, D_MODEL)), const((D_DN, D_MODEL))]
    args = [conv, cgs, o_f, o_b, zs, x, cb, lnw, lnb, dnw, gate, wc, wd]
    if final:
        in_specs.append(const((1, D_MODEL)))
        args.append(fnw)
    return pl.pallas_call(
        functools.partial(_outproj_kernel, final=final),
        grid=(bsz, t // tm),
        in_specs=in_specs,
        out_specs=row(D_MODEL),
        out_shape=jax.ShapeDtypeStruct((bsz, t, D_MODEL), F32),
        compiler_params=pltpu.CompilerParams(
            dimension_semantics=("parallel", "parallel"), vmem_limit_bytes=VMEM_LIMIT),
        name="outproj_final" if final else "outproj",
    )(*args)


def _layer_weights(l, norm_w, w_in, conv_w, conv_b, conv_ln_w, conv_ln_b, short_conv_w, a_log,
                   dt_bias, dn_norm_w, w_out):
    wi = w_in[l]
    c0 = 3 * D_CONV
    w_ab = jnp.zeros((D_MODEL, LANES), F32).at[:, :4 * DN_HEADS].set(wi[:, c0 + 4 * D_DN:])
    pad_vec = lambda v: v.reshape(2 * DN_HEADS, 1)
    return dict(
        nw=norm_w[l].reshape(1, D_MODEL),
        w_full=[wi[:, :D_CONV].astype(BF16), wi[:, D_CONV:2 * D_CONV].astype(BF16),
                wi[:, 2 * D_CONV:c0].astype(BF16), wi[:, c0 + 3 * D_DN:c0 + 4 * D_DN].astype(BF16)],
        w_dn=[wi[:, c0:c0 + 3 * D_DN].astype(BF16), w_ab.astype(BF16)],
        alog=pad_vec(a_log[l]), dtb=pad_vec(dt_bias[l]),
        conv_w=jnp.zeros((4 * SUBLANES, D_CONV), F32).at[:CONV_W].set(conv_w[l]),
        conv_b=conv_b[l].reshape(1, D_CONV), lnw=conv_ln_w[l].reshape(1, D_CONV),
        lnb=conv_ln_b[l].reshape(1, D_CONV),
        scw=jnp.zeros((SUBLANES, 3 * D_DN), F32).at[:SHORT_CONV_W].set(short_conv_w[l]),
        dnw=dn_norm_w[l].reshape(1, DN_HEAD_DIM),
        wc=w_out[l, :D_CONV].astype(BF16), wd=w_out[l, D_CONV:].astype(BF16),
    )


def _mixer(x, mod, p, s0_f, s0_b, *, grid_mode, tm, fnw=None, states_only=False):
    bsz = x.shape[0]
    shift, scale, gate = (mod[:, j * D_MODEL:(j + 1) * D_MODEL].reshape(bsz, 1, D_MODEL) for j in range(3))
    if states_only:
        q, k, v, gb = _inproj(x, p["nw"], shift, scale, p["w_dn"], p["alog"], p["dtb"], p["scw"],
                              full=False, tm=tm)
        _, s_f = _delta(q, k, v, gb, s0_f, reverse=False)
        _, s_b = _delta(q, k, v, gb, s0_b, reverse=True)
        return x, s_f, s_b
    u, cgs, zs, q, k, v, gb = _inproj(x, p["nw"], shift, scale, p["w_full"] + p["w_dn"], p["alog"],
                                      p["dtb"], p["scw"], full=True, tm=tm)
    conv = _dwconv(u, p["conv_w"], grid_mode=grid_mode)
    o_f, s_f = _delta(q, k, v, gb, s0_f, reverse=False)
    o_b, s_b = _delta(q, k, v, gb, s0_b, reverse=True)
    x_new = _outproj(conv, cgs, o_f, o_b, zs, x, p["conv_b"], p["lnw"], p["lnb"], p["dnw"], gate,
                     p["wc"], p["wd"], fnw, tm=min(x.shape[1], OUT_TILE))
    return x_new, s_f, s_b


def kernel(x, c, ctx, c_ctx, norm_w, w_ada, b_ada, w_in, conv_w, conv_b, conv_ln_w, conv_ln_b, short_conv_w, a_log, dt_bias, dn_norm_w, w_out, final_norm_w):
    bsz = x.shape[0]
    cond_t = jnp.zeros((D_MODEL, LANES), F32).at[:, :bsz].set(c.T).at[:, bsz].set(c_ctx)
    mods = _ada_all(cond_t, w_ada, b_ada, nrows=bsz + 1)
    zeros = jnp.zeros((bsz, DN_HEADS, DN_HEAD_DIM, DN_HEAD_DIM), F32)
    for l in range(DEPTH):
        p = _layer_weights(l, norm_w, w_in, conv_w, conv_b, conv_ln_w, conv_ln_b, short_conv_w,
                           a_log, dt_bias, dn_norm_w, w_out)
        mod_ctx = jnp.broadcast_to(mods[l, bsz:bsz + 1], (bsz, 3 * D_MODEL))
        last = l == DEPTH - 1
        ctx_new, s_f, s_b = _mixer(ctx, mod_ctx, p, zeros, zeros, grid_mode=False, tm=ctx.shape[1],
                                   states_only=last)
        x, _, _ = _mixer(x, mods[l, :bsz], p, s_f, s_b, grid_mode=True, tm=IN_TILE,
                         fnw=final_norm_w.reshape(1, D_MODEL) if last else None)
        ctx = ctx_new
    return x
```

```python
import functools

import jax
import jax.numpy as jnp
from jax import lax
from jax.experimental import pallas as pl
from jax.experimental.pallas import tpu as pltpu

D_MODEL = 1024
DEPTH = 4
GRID_W = 64
D_CONV = 512
CONV_W = 31
CONV_PAD = (CONV_W - 1) // 2
DN_HEADS = 4
DN_HEAD_DIM = 128
D_DN = DN_HEADS * DN_HEAD_DIM
SHORT_CONV_W = 5
SHORT_PAD = (SHORT_CONV_W - 1) // 2
CHUNK = 64
EPS = 1e-6
LN_EPS = 1e-5

LANES = 128
SUBLANES = 8
IN_TILE = 1024
OUT_TILE = 1024
DELTA_BLOCK = 512
SOLVE_BLOCK = 128
INV_BASE = 16
BF16_ROWS = 16
HALO = BF16_ROWS
CONV_UNROLL = 4
SEG_GAP = 16
VMEM_LIMIT = 48 * 1024 * 1024

F32 = jnp.float32
BF16 = jnp.bfloat16


def _dot(a, b):
    return jnp.dot(a, b, preferred_element_type=F32)


def _dot_nt(a, b):
    return lax.dot_general(a, b, (((1,), (1,)), ((), ())), preferred_element_type=F32)


def _sigmoid(x):
    return 0.5 * jnp.tanh(0.5 * x) + 0.5


def _silu(x):
    half = 0.5 * x
    return half * jnp.tanh(half) + half


def _softplus(x):
    return jnp.maximum(x, 0.0) + jnp.log(1.0 + jnp.exp(-jnp.abs(x)))


def _dot_mask_f32(x, mask_bf16):
    hi = x.astype(BF16)
    r1 = x - hi.astype(F32)
    mid = r1.astype(BF16)
    lo = (r1 - mid.astype(F32)).astype(BF16)
    return _dot(hi, mask_bf16) + _dot(mid, mask_bf16) + _dot(lo, mask_bf16)


def _ada_kernel(cond_t_ref, w_ref, b_ref, o_ref, *, nrows):
    s_t = _silu(cond_t_ref[...])
    w = w_ref[0]
    rows = [jnp.sum(s_t[:, r:r + 1] * w, axis=0, keepdims=True) for r in range(nrows)]
    rows.append(jnp.zeros((SUBLANES - nrows, w.shape[1]), F32))
    o_ref[0] = jnp.concatenate(rows, axis=0) + b_ref[0]


def _ada_all(cond_t, w_ada, b_ada, *, nrows):
    tn = 3 * D_MODEL
    return pl.pallas_call(
        functools.partial(_ada_kernel, nrows=nrows),
        grid=(DEPTH, 3 * D_MODEL // tn),
        in_specs=[
            pl.BlockSpec((D_MODEL, LANES), lambda l, j: (0, 0)),
            pl.BlockSpec((1, D_MODEL, tn), lambda l, j: (l, 0, j)),
            pl.BlockSpec((1, 1, tn), lambda l, j: (l, 0, j)),
        ],
        out_specs=pl.BlockSpec((1, SUBLANES, tn), lambda l, j: (l, 0, j)),
        out_shape=jax.ShapeDtypeStruct((DEPTH, SUBLANES, 3 * D_MODEL), F32),
        compiler_params=pltpu.CompilerParams(vmem_limit_bytes=VMEM_LIMIT),
        name="ada",
    )(cond_t, w_ada, b_ada.reshape(DEPTH, 1, 3 * D_MODEL))


def _inproj_kernel(*refs, full, tm, nblk):
    if full:
        (prev_ref, x_ref, next_ref, nw_ref, shift_ref, scale_ref, wa_ref, wb_ref, wcg_ref, wz_ref,
         wqkv_ref, wab_ref, alog_ref, dtb_ref, scw_ref,
         u_ref, cgs_ref, zs_ref, q_ref, k_ref, v_ref, gb_ref, h_scr, qkv_scr) = refs
    else:
        (prev_ref, x_ref, next_ref, nw_ref, shift_ref, scale_ref, wqkv_ref, wab_ref, alog_ref, dtb_ref,
         scw_ref, q_ref, k_ref, v_ref, gb_ref, h_scr, qkv_scr) = refs
    i = pl.program_id(1)
    gain = nw_ref[...] * (1.0 + scale_ref[0])

    def modulated(x):
        y = x * lax.rsqrt(jnp.mean(x * x, axis=-1, keepdims=True) + EPS)
        return (y * gain + shift_ref[0]).astype(BF16)

    h_scr[pl.ds(0, HALO), :] = modulated(prev_ref[0])
    h_scr[pl.ds(HALO, tm), :] = modulated(x_ref[0])
    h_scr[pl.ds(HALO + tm, HALO), :] = modulated(next_ref[0])

    def h_dot(w_ref):
        return _dot(h_scr[pl.ds(HALO, tm), :], w_ref[...])

    qkv = _dot(h_scr[...], wqkv_ref[...])
    for grp in range(3 * DN_HEADS):
        cols = slice(grp * LANES, (grp + 1) * LANES)
        qkv_scr[grp, pl.ds(0, HALO), :] = jnp.where(i > 0, qkv[:HALO, cols], 0.0)
        qkv_scr[grp, pl.ds(HALO, tm), :] = qkv[HALO:HALO + tm, cols]
        qkv_scr[grp, pl.ds(HALO + tm, HALO), :] = jnp.where(i < nblk - 1, qkv[HALO + tm:, cols], 0.0)

    def conv_silu(grp):
        acc = jnp.zeros((tm, LANES), F32)
        for j in range(SHORT_CONV_W):
            acc = acc + (qkv_scr[grp, pl.ds(HALO - SHORT_PAD + j, tm), :]
                         * scw_ref[j:j + 1, grp * LANES:(grp + 1) * LANES])
        return _silu(acc)

    def l2n(m):
        return m * lax.rsqrt(jnp.sum(m * m, axis=-1, keepdims=True) + EPS)

    def qkv_epilogue(hd):
        sl = slice(hd * LANES, (hd + 1) * LANES)
        q_ref[0, :, sl] = (l2n(conv_silu(hd)) * (DN_HEAD_DIM ** -0.5)).astype(q_ref.dtype)
        k_ref[0, :, sl] = l2n(conv_silu(DN_HEADS + hd)).astype(k_ref.dtype)
        v_ref[0, :, sl] = conv_silu(2 * DN_HEADS + hd).astype(v_ref.dtype)

    if full:
        u_ref[0] = h_dot(wa_ref) * _sigmoid(h_dot(wb_ref))
        qkv_epilogue(0)
        cgs_ref[0] = _silu(h_dot(wcg_ref)).astype(cgs_ref.dtype)
        qkv_epilogue(1)
        zs_ref[0] = _silu(h_dot(wz_ref)).astype(zs_ref.dtype)
        qkv_epilogue(2)
    else:
        for hd in range(DN_HEADS - 1):
            qkv_epilogue(hd)
    ab_t = h_dot(wab_ref).T
    g_t = -jnp.exp(alog_ref[...]) * _softplus(ab_t[0:2 * DN_HEADS] + dtb_ref[...])
    gb_ref[0] = jnp.concatenate([g_t, _sigmoid(ab_t[2 * DN_HEADS:4 * DN_HEADS])], axis=0)
    qkv_epilogue(DN_HEADS - 1)


def _inproj(x, nw, shift, scale, wts, alog, dtb, scw, *, full, tm):
    bsz, t, _ = x.shape
    nblk = t // tm
    halo_per_blk = tm // HALO
    last_halo = t // HALO - 1
    row = lambda n: pl.BlockSpec((1, tm, n), lambda b, i: (b, i, 0))
    const = lambda shp: pl.BlockSpec(shp, lambda b, i: (0,) * len(shp))
    per_b = pl.BlockSpec((1, 1, D_MODEL), lambda b, i: (b, 0, 0))
    halo_prev = pl.BlockSpec((1, HALO, D_MODEL),
                             lambda b, i: (b, jnp.maximum(i * halo_per_blk - 1, 0), 0))
    halo_next = pl.BlockSpec((1, HALO, D_MODEL),
                             lambda b, i: (b, jnp.minimum((i + 1) * halo_per_blk, last_halo), 0))
    w_specs = [pl.BlockSpec(w.shape, lambda b, i: (0, 0), pipeline_mode=pl.Buffered(1)) for w in wts]
    in_specs = ([halo_prev, row(D_MODEL), halo_next, const((1, D_MODEL)), per_b, per_b] + w_specs
                + [const((2 * DN_HEADS, 1))] * 2 + [const((SUBLANES, 3 * D_DN))])
    outs = ([(D_CONV, F32), (D_CONV, BF16), (D_DN, BF16)] if full else []) + [
        (D_DN, BF16), (D_DN, BF16), (D_DN, BF16)]
    gb_spec = pl.BlockSpec((1, 4 * DN_HEADS, tm), lambda b, i: (b, 0, i))
    gb_shape = jax.ShapeDtypeStruct((bsz, 4 * DN_HEADS, t), F32)
    return pl.pallas_call(
        functools.partial(_inproj_kernel, full=full, tm=tm, nblk=nblk),
        grid=(bsz, nblk),
        in_specs=in_specs,
        out_specs=[row(n) for n, _ in outs] + [gb_spec],
        out_shape=[jax.ShapeDtypeStruct((bsz, t, n), dt) for n, dt in outs] + [gb_shape],
        scratch_shapes=[pltpu.VMEM((tm + 2 * HALO, D_MODEL), BF16),
                        pltpu.VMEM((3 * DN_HEADS, tm + 2 * HALO, LANES), F32)],
        compiler_params=pltpu.CompilerParams(
            dimension_semantics=("parallel", "parallel"), vmem_limit_bytes=VMEM_LIMIT),
        name="inproj_full" if full else "inproj_dn",
    )(x, x, x, nw, shift, scale, *wts, alog, dtb, scw)


def _conv_segments(u_ref, w_ref, o_ref, pad_ref, *, seg, nseg):
    stride = seg + SEG_GAP
    zeros_gap = jnp.zeros((SEG_GAP, LANES), F32)

    def fill(r, carry):
        base = pl.multiple_of(r * stride, SUBLANES)
        src = pl.multiple_of(r * seg, SUBLANES)
        pad_ref[pl.ds(base, SEG_GAP), :] = zeros_gap
        pad_ref[pl.ds(base + SEG_GAP, seg), :] = u_ref[0, pl.ds(src, seg), :]
        return carry

    lax.fori_loop(0, nseg, fill, 0)
    pad_ref[pl.ds(nseg * stride, SEG_GAP), :] = zeros_gap

    def body(r, carry):
        base = pl.multiple_of(r * stride, SUBLANES)
        dst = pl.multiple_of(r * seg, BF16_ROWS)
        acc = jnp.zeros((seg, LANES), F32)
        for k in range(CONV_W):
            acc = acc + pad_ref[pl.ds(base + (SEG_GAP - CONV_PAD) + k, seg), :] * w_ref[k:k + 1, :]
        o_ref[0, pl.ds(dst, seg), :] = acc.astype(o_ref.dtype)
        return carry

    lax.fori_loop(0, nseg, body, 0, unroll=CONV_UNROLL if nseg % CONV_UNROLL == 0 else 1)


def _conv_strided(u_ref, w_ref, o_ref, pad_ref, *, t, step):
    halo = CONV_PAD * step
    pad_ref[pl.ds(0, halo), :] = jnp.zeros((halo, LANES), F32)
    pad_ref[pl.ds(halo + t, halo), :] = jnp.zeros((halo, LANES), F32)
    pad_ref[pl.ds(halo, t), :] = u_ref[0]

    def body(r, carry):
        base = pl.multiple_of(r * step, BF16_ROWS)
        acc = jnp.zeros((step, LANES), F32)
        for k in range(CONV_W):
            acc = acc + pad_ref[pl.ds(base + k * step, step), :] * w_ref[k:k + 1, :]
        o_ref[0, pl.ds(base, step), :] = acc.astype(o_ref.dtype)
        return carry

    lax.fori_loop(0, t // step, body, 0, unroll=CONV_UNROLL if (t // step) % CONV_UNROLL == 0 else 1)


def _grid_conv_kernel(u_ref, w_ref, o_ref, pad_ref, *, t):
    j = pl.program_id(1)
    half_blocks = D_CONV // LANES // 2

    @pl.when(j < half_blocks)
    def _():
        _conv_segments(u_ref, w_ref, o_ref, pad_ref, seg=GRID_W, nseg=t // GRID_W)

    @pl.when(j >= half_blocks)
    def _():
        _conv_strided(u_ref, w_ref, o_ref, pad_ref, t=t, step=GRID_W)


def _seq_conv_kernel(u_ref, w_ref, o_ref, pad_ref, *, t):
    _conv_segments(u_ref, w_ref, o_ref, pad_ref, seg=t, nseg=1)


def _dwconv(u, w_pad, *, grid_mode):
    bsz, t, _ = u.shape
    if grid_mode:
        rows = t // GRID_W
        pad_rows = max(rows * (GRID_W + SEG_GAP) + SEG_GAP, t + 2 * CONV_PAD * GRID_W)
        body = functools.partial(_grid_conv_kernel, t=t)
    else:
        pad_rows = t + 2 * SEG_GAP
        body = functools.partial(_seq_conv_kernel, t=t)
    return pl.pallas_call(
        body,
        grid=(bsz, D_CONV // LANES),
        in_specs=[
            pl.BlockSpec((1, t, LANES), lambda b, j: (b, 0, j)),
            pl.BlockSpec((4 * SUBLANES, LANES), lambda b, j: (0, j)),
        ],
        out_specs=pl.BlockSpec((1, t, LANES), lambda b, j: (b, 0, j)),
        out_shape=jax.ShapeDtypeStruct((bsz, t, D_CONV), BF16),
        scratch_shapes=[pltpu.VMEM((pad_rows, LANES), F32)],
        compiler_params=pltpu.CompilerParams(
            dimension_semantics=("parallel", "parallel"), vmem_limit_bytes=VMEM_LIMIT),
        name="grid_conv" if grid_mode else "seq_conv",
    )(u, w_pad)


def _delta_kernel(*refs, reverse, nblk, nb, add_other):
    if add_other:
        (q_ref, k_ref, v_ref, gb_ref, s0_ref, other_ref, o_ref, sfin_ref,
         s_scr, u_scr, wq_scr, ak_scr, gl_scr) = refs
    else:
        (q_ref, k_ref, v_ref, gb_ref, s0_ref, o_ref, sfin_ref,
         s_scr, u_scr, wq_scr, ak_scr, gl_scr) = refs
    i = pl.program_id(1)
    heads = range(DN_HEADS)
    nchunk = nb // CHUNK

    @pl.when(i == 0)
    def _():
        s_scr[...] = s0_ref[0]
        u_scr[...] = jnp.zeros(u_scr.shape, F32)
        wq_scr[...] = jnp.zeros(wq_scr.shape, BF16)
        ak_scr[...] = jnp.zeros(ak_scr.shape, BF16)
        gl_scr[...] = jnp.ones(gl_scr.shape, F32)

    def recurrence():
        s = [s_scr[h] for h in heads]
        for step in range(nchunk):
            c = (nchunk - 1 - step) if reverse else step
            r = [_dot(wq_scr[h, c], s[h].astype(BF16)) for h in heads]
            yield
            v_new = [(u_scr[h, pl.ds(c * CHUNK, CHUNK), :] - r[h][:CHUNK]).astype(BF16) for h in heads]
            av = [_dot(ak_scr[h, c], v_new[h]) for h in heads]
            for h in heads:
                o_c = av[h][:CHUNK] + r[h][CHUNK:]
                if add_other:
                    o_c = o_c + other_ref[0, pl.ds(c * CHUNK, CHUNK), h * LANES:(h + 1) * LANES].astype(F32)
                o_ref[0, pl.ds(c * CHUNK, CHUNK), h * LANES:(h + 1) * LANES] = o_c.astype(o_ref.dtype)
            s = [av[h][CHUNK:]
                 + s[h] * jnp.concatenate([gl_scr[h, c]] * (DN_HEAD_DIM // SUBLANES), axis=0)
                 for h in heads]
            yield
        for h in heads:
            s_scr[h] = s[h]
        while True:
            yield

    rec = recurrence()
    half_steps_per_stage = -(-(2 * nchunk + 1) // 10)

    def advance():
        for _ in range(half_steps_per_stage):
            next(rec)

    advance()

    sb = SOLVE_BLOCK
    units = [(h, hf) for h in heads for hf in range(nb // sb)]
    nu = range(len(units))
    lgs = [(DN_HEADS if reverse else 0) + h for h, _ in units]
    tok = [slice(hf * sb, (hf + 1) * sb) for _, hf in units]
    qbf = [q_ref[0, tok[u], h * LANES:(h + 1) * LANES] for u, (h, _) in enumerate(units)]
    kbf = [k_ref[0, tok[u], h * LANES:(h + 1) * LANES] for u, (h, _) in enumerate(units)]

    def chunk_masks(n_tok):
        row = lax.broadcasted_iota(jnp.int32, (n_tok, n_tok), 0)
        col = lax.broadcasted_iota(jnp.int32, (n_tok, n_tok), 1)
        same = (row // CHUNK) == (col // CHUNK)
        before = (col > row) if reverse else (col < row)
        return row, col, same, before

    _, _, same_nb, before_nb = chunk_masks(nb)
    gbt = gb_ref[0]
    nrow = 4 * DN_HEADS
    cum_mask = jnp.logical_and(same_nb, jnp.logical_not(before_nb))
    gc_t = _dot_mask_f32(gbt, jnp.where(cum_mask, 1.0, 0.0).astype(BF16))
    tot_t = _dot_mask_f32(gbt, jnp.where(same_nb, 1.0, 0.0).astype(BF16))
    cols = jnp.concatenate([gbt, gc_t, tot_t, jnp.zeros((LANES - 3 * nrow, nb), F32)], axis=0).T
    advance()

    row, col, same, before = chunk_masks(sb)
    strict = jnp.logical_and(same, before)
    incl = jnp.logical_or(strict, row == col)
    base_blocks = (row // INV_BASE) == (col // INV_BASE)

    q = [qbf[u].astype(F32) for u in nu]
    k = [kbf[u].astype(F32) for u in nu]
    v = [v_ref[0, tok[u], h * LANES:(h + 1) * LANES].astype(F32) for u, (h, _) in enumerate(units)]
    beta = [cols[tok[u], 2 * DN_HEADS + lgs[u]:2 * DN_HEADS + lgs[u] + 1] for u in nu]
    gcol = [cols[tok[u], nrow + lgs[u]:nrow + lgs[u] + 1] for u in nu]
    tcol = [cols[tok[u], 2 * nrow + lgs[u]:2 * nrow + lgs[u] + 1] for u in nu]
    decay = [jnp.where(incl, jnp.exp(jnp.where(incl, gcol[u] - gc_t[lgs[u]:lgs[u] + 1, tok[u]], 0.0)), 0.0)
             for u in nu]
    kb = [k[u] * beta[u] for u in nu]
    eg = [jnp.exp(gcol[u]) for u in nu]
    lfull = [_dot_nt(kb[u].astype(BF16), kbf[u]) * jnp.where(strict, decay[u], 0.0) for u in nu]
    advance()
    l0 = [jnp.where(base_blocks, lfull[u], 0.0) for u in nu]
    pb = [l0[u].astype(BF16) for u in nu]
    n = [-l0[u] for u in nu]
    nbf = [(-l0[u]).astype(BF16) for u in nu]
    doublings = INV_BASE.bit_length() - 2
    for it in range(doublings):
        pw = [_dot(pb[u], pb[u]) for u in nu]
        n_plus = [n[u] + pw[u] for u in nu]
        pb = [pw[u].astype(BF16) for u in nu]
        advance()
        n = [_dot(nbf[u], pb[u]) + n_plus[u] for u in nu]
        nbf = [n[u].astype(BF16) for u in nu]
        advance()
    size = INV_BASE
    while size < CHUNK:
        picked = [j for j in range(sb // size) if j % 2 == (0 if reverse else 1)]

        def take(m):
            return jnp.concatenate([m[j * size:(j + 1) * size] for j in picked], axis=0)

        def put(full_blocks, picked_rows):
            blocks = list(full_blocks)
            for idx, j in enumerate(picked):
                blocks[j] = picked_rows[idx * size:(idx + 1) * size]
            return jnp.concatenate(blocks, axis=0)

        prow = lax.broadcasted_iota(jnp.int32, (sb // 2, sb), 0)
        pcol = lax.broadcasted_iota(jnp.int32, (sb // 2, sb), 1)
        orig = (prow // size) * (2 * size) + (0 if reverse else size) + prow % size
        pair_mask = jnp.logical_and(orig // (2 * size) == pcol // (2 * size), orig // size != pcol // size)
        zero_blocks = [jnp.zeros((size, sb), BF16)] * (sb // size)
        cm = [jnp.where(pair_mask, take(lfull[u]), 0.0) for u in nu]
        nbf = [n[u].astype(BF16) for u in nu]
        xm = [_dot(cm[u].astype(BF16), nbf[u]) + cm[u] for u in nu]
        advance()
        ym = [_dot(take(nbf[u]), put(zero_blocks, xm[u].astype(BF16))) + xm[u] for u in nu]
        n = [put([n[u][j * size:(j + 1) * size] for j in range(sb // size)], take(n[u]) - ym[u])
             for u in nu]
        advance()
        size *= 2
    rhs = [jnp.concatenate([v[u] * beta[u], kb[u] * eg[u]], axis=-1) for u in nu]
    y = [_dot(n[u].astype(BF16), rhs[u].astype(BF16)) + rhs[u] for u in nu]
    advance()
    attn = [_dot_nt(qbf[u], kbf[u]) * decay[u] for u in nu]
    for _ in range(2 * nchunk + 1):
        next(rec)
    for u, (h, hf) in enumerate(units):
        kt_t = (k[u] * jnp.exp(tcol[u] - gcol[u])).T
        qg = q[u] * eg[u]
        u_scr[h, pl.ds(hf * sb, sb), :] = y[u][:, :LANES]
        for cc in range(sb // CHUNK):
            c = hf * (sb // CHUNK) + cc
            sl = slice(cc * CHUNK, (cc + 1) * CHUNK)
            wq_scr[h, c, 0:CHUNK, :] = y[u][sl, LANES:].astype(BF16)
            wq_scr[h, c, CHUNK:2 * CHUNK, :] = qg[sl, :].astype(BF16)
            ak_scr[h, c, 0:CHUNK, :] = attn[u][sl, sl].astype(BF16)
            ak_scr[h, c, CHUNK:, :] = kt_t[:, sl].astype(BF16)
            gl_scr[h, c] = jnp.broadcast_to(
                jnp.exp(tot_t[lgs[u]:lgs[u] + 1, c * CHUNK:c * CHUNK + 1]), (SUBLANES, LANES))

    @pl.when(i == nblk)
    def _():
        sfin_ref[0] = s_scr[...]


def _delta(q, k, v, gb, s0, *, reverse, other=None):
    bsz, t, _ = q.shape
    nb = min(DELTA_BLOCK, t)
    nblk = t // nb
    nchunk = nb // CHUNK
    if reverse:
        bidx = lambda i: nblk - 1 - i
    else:
        bidx = lambda i: i
    blk = lambda n: pl.BlockSpec((1, nb, n), lambda b, i: (b, bidx(jnp.minimum(i, nblk - 1)), 0))
    gb_blk = pl.BlockSpec((1, 4 * DN_HEADS, nb), lambda b, i: (b, 0, bidx(jnp.minimum(i, nblk - 1))))
    out_blk = pl.BlockSpec((1, nb, D_DN), lambda b, i: (b, bidx(jnp.maximum(i - 1, 0)), 0))
    state_spec = pl.BlockSpec((1, DN_HEADS, DN_HEAD_DIM, DN_HEAD_DIM), lambda b, i: (b, 0, 0, 0))
    in_specs = [blk(D_DN), blk(D_DN), blk(D_DN), gb_blk, state_spec]
    args = [q, k, v, gb, s0]
    if other is not None:
        in_specs.append(out_blk)
        args.append(other)
    return pl.pallas_call(
        functools.partial(_delta_kernel, reverse=reverse, nblk=nblk, nb=nb, add_other=other is not None),
        grid=(bsz, nblk + 1),
        in_specs=in_specs,
        out_specs=[out_blk, state_spec],
        out_shape=[
            jax.ShapeDtypeStruct((bsz, t, D_DN), BF16),
            jax.ShapeDtypeStruct((bsz, DN_HEADS, DN_HEAD_DIM, DN_HEAD_DIM), F32),
        ],
        scratch_shapes=[
            pltpu.VMEM((DN_HEADS, DN_HEAD_DIM, DN_HEAD_DIM), F32),
            pltpu.VMEM((DN_HEADS, nb, LANES), F32),
            pltpu.VMEM((DN_HEADS, nchunk, 2 * CHUNK, LANES), BF16),
            pltpu.VMEM((DN_HEADS, nchunk, CHUNK + DN_HEAD_DIM, CHUNK), BF16),
            pltpu.VMEM((DN_HEADS, nchunk, SUBLANES, LANES), F32),
        ],
        compiler_params=pltpu.CompilerParams(
            dimension_semantics=("arbitrary", "arbitrary"), vmem_limit_bytes=VMEM_LIMIT),
        name="delta_bwd" if reverse else "delta_fwd",
    )(*args)


def _outproj_kernel(*refs, final):
    if final:
        (conv_ref, cgs_ref, odn_ref, zs_ref, x_ref, cb_ref, lnw_ref, lnb_ref, dnw_ref,
         gate_ref, wc_ref, wd_ref, fnw_ref, o_ref) = refs
    else:
        (conv_ref, cgs_ref, odn_ref, zs_ref, x_ref, cb_ref, lnw_ref, lnb_ref, dnw_ref,
         gate_ref, wc_ref, wd_ref, o_ref) = refs
    yc = conv_ref[0].astype(F32) + cb_ref[...]
    mu = jnp.mean(yc, axis=-1, keepdims=True)
    d = yc - mu
    var = jnp.mean(d * d, axis=-1, keepdims=True)
    yn = d * lax.rsqrt(var + LN_EPS) * lnw_ref[...] + lnb_ref[...]
    y_conv = (_silu(yn) * cgs_ref[0].astype(F32)).astype(BF16)
    o = odn_ref[0].astype(F32)
    acc = _dot(y_conv, wc_ref[...])
    for h in range(DN_HEADS):
        sl = slice(h * LANES, (h + 1) * LANES)
        oh = o[:, sl]
        on = oh * lax.rsqrt(jnp.mean(oh * oh, axis=-1, keepdims=True) + EPS) * dnw_ref[...]
        y_dn = (on * zs_ref[0, :, sl].astype(F32)).astype(BF16)
        acc = acc + _dot(y_dn, wd_ref[sl, :])
    xn = x_ref[0] + gate_ref[0] * acc
    if final:
        xn = xn * lax.rsqrt(jnp.mean(xn * xn, axis=-1, keepdims=True) + EPS) * fnw_ref[...]
    o_ref[0] = xn


def _outproj(conv, cgs, o_dn, zs, x, cb, lnw, lnb, dnw, gate, wc, wd, fnw, *, tm):
    bsz, t, _ = x.shape
    final = fnw is not None
    row = lambda n: pl.BlockSpec((1, tm, n), lambda b, i: (b, i, 0))
    const = lambda shp: pl.BlockSpec(shp, lambda b, i: (0,) * len(shp))
    per_b = pl.BlockSpec((1, 1, D_MODEL), lambda b, i: (b, 0, 0))
    in_specs = [row(D_CONV), row(D_CONV), row(D_DN), row(D_DN), row(D_MODEL),
                const((1, D_CONV)), const((1, D_CONV)), const((1, D_CONV)), const((1, DN_HEAD_DIM)),
                per_b, const((D_CONV, D_MODEL)), const((D_DN, D_MODEL))]
    args = [conv, cgs, o_dn, zs, x, cb, lnw, lnb, dnw, gate, wc, wd]
    if final:
        in_specs.append(const((1, D_MODEL)))
        args.append(fnw)
    return pl.pallas_call(
        functools.partial(_outproj_kernel, final=final),
        grid=(bsz, t // tm),
        in_specs=in_specs,
        out_specs=row(D_MODEL),
        out_shape=jax.ShapeDtypeStruct((bsz, t, D_MODEL), F32),
        compiler_params=pltpu.CompilerParams(
            dimension_semantics=("parallel", "parallel"), vmem_limit_bytes=VMEM_LIMIT),
        name="outproj_final" if final else "outproj",
    )(*args)


def _layer_weights(l, norm_w, w_in, conv_w, conv_b, conv_ln_w, conv_ln_b, short_conv_w, a_log,
                   dt_bias, dn_norm_w, w_out):
    wi = w_in[l]
    c0 = 3 * D_CONV
    w_ab = jnp.zeros((D_MODEL, LANES), F32).at[:, :4 * DN_HEADS].set(wi[:, c0 + 4 * D_DN:])
    pad_vec = lambda v: v.reshape(2 * DN_HEADS, 1)
    return dict(
        nw=norm_w[l].reshape(1, D_MODEL),
        w_full=[wi[:, :D_CONV].astype(BF16), wi[:, D_CONV:2 * D_CONV].astype(BF16),
                wi[:, 2 * D_CONV:c0].astype(BF16), wi[:, c0 + 3 * D_DN:c0 + 4 * D_DN].astype(BF16)],
        w_dn=[wi[:, c0:c0 + 3 * D_DN].astype(BF16), w_ab.astype(BF16)],
        alog=pad_vec(a_log[l]), dtb=pad_vec(dt_bias[l]),
        conv_w=jnp.zeros((4 * SUBLANES, D_CONV), F32).at[:CONV_W].set(conv_w[l]),
        conv_b=conv_b[l].reshape(1, D_CONV), lnw=conv_ln_w[l].reshape(1, D_CONV),
        lnb=conv_ln_b[l].reshape(1, D_CONV),
        scw=jnp.zeros((SUBLANES, 3 * D_DN), F32).at[:SHORT_CONV_W].set(short_conv_w[l]),
        dnw=dn_norm_w[l].reshape(1, DN_HEAD_DIM),
        wc=w_out[l, :D_CONV].astype(BF16), wd=w_out[l, D_CONV:].astype(BF16),
    )


def _mixer(x, mod, p, s0_f, s0_b, *, grid_mode, tm, fnw=None, states_only=False):
    bsz = x.shape[0]
    shift, scale, gate = (mod[:, j * D_MODEL:(j + 1) * D_MODEL].reshape(bsz, 1, D_MODEL) for j in range(3))
    if states_only:
        q, k, v, gb = _inproj(x, p["nw"], shift, scale, p["w_dn"], p["alog"], p["dtb"], p["scw"],
                              full=False, tm=tm)
        _, s_f = _delta(q, k, v, gb, s0_f, reverse=False)
        _, s_b = _delta(q, k, v, gb, s0_b, reverse=True)
        return x, s_f, s_b
    u, cgs, zs, q, k, v, gb = _inproj(x, p["nw"], shift, scale, p["w_full"] + p["w_dn"], p["alog"],
                                      p["dtb"], p["scw"], full=True, tm=tm)
    conv = _dwconv(u, p["conv_w"], grid_mode=grid_mode)
    o_f, s_f = _delta(q, k, v, gb, s0_f, reverse=False)
    o_dn, s_b = _delta(q, k, v, gb, s0_b, reverse=True, other=o_f)
    x_new = _outproj(conv, cgs, o_dn, zs, x, p["conv_b"], p["lnw"], p["lnb"], p["dnw"], gate,
                     p["wc"], p["wd"], fnw, tm=min(x.shape[1], OUT_TILE))
    return x_new, s_f, s_b


def kernel(x, c, ctx, c_ctx, norm_w, w_ada, b_ada, w_in, conv_w, conv_b, conv_ln_w, conv_ln_b, short_conv_w, a_log, dt_bias, dn_norm_w, w_out, final_norm_w):
    bsz = x.shape[0]
    cond_t = jnp.zeros((D_MODEL, LANES), F32).at[:, :bsz].set(c.T).at[:, bsz].set(c_ctx)
    mods = _ada_all(cond_t, w_ada, b_ada, nrows=bsz + 1)
    zeros = jnp.zeros((bsz, DN_HEADS, DN_HEAD_DIM, DN_HEAD_DIM), F32)
    for l in range(DEPTH):
        p = _layer_weights(l, norm_w, w_in, conv_w, conv_b, conv_ln_w, conv_ln_b, short_conv_w,
                           a_log, dt_bias, dn_norm_w, w_out)
        mod_ctx = jnp.broadcast_to(mods[l, bsz:bsz + 1], (bsz, 3 * D_MODEL))
        last = l == DEPTH - 1
        ctx_new, s_f, s_b = _mixer(ctx, mod_ctx, p, zeros, zeros, grid_mode=False, tm=ctx.shape[1],
                                   states_only=last)
        x, _, _ = _mixer(x, mods[l, :bsz], p, s_f, s_b, grid_mode=True, tm=IN_TILE,
                         fnw=final_norm_w.reshape(1, D_MODEL) if last else None)
        ctx = ctx_new
    return x
```

```python
import functools

import jax
import jax.numpy as jnp
from jax import lax
from jax.experimental import pallas as pl
from jax.experimental.pallas import tpu as pltpu

D_MODEL = 1024
DEPTH = 4
GRID_W = 64
D_CONV = 512
CONV_W = 31
CONV_PAD = (CONV_W - 1) // 2
DN_HEADS = 4
DN_HEAD_DIM = 128
D_DN = DN_HEADS * DN_HEAD_DIM
SHORT_CONV_W = 5
SHORT_PAD = (SHORT_CONV_W - 1) // 2
CHUNK = 64
EPS = 1e-6
LN_EPS = 1e-5

LANES = 128
SUBLANES = 8
IN_TILE = 1024
OUT_TILE = 1024
DELTA_BLOCK = 512
SOLVE_BLOCK = 128
INV_BASE = 16
BF16_ROWS = 16
HALO = BF16_ROWS
CONV_UNROLL = 4
SEG_GAP = 16
VMEM_LIMIT = 48 * 1024 * 1024

F32 = jnp.float32
BF16 = jnp.bfloat16


def _dot(a, b):
    return jnp.dot(a, b, preferred_element_type=F32)


def _dot_nt(a, b):
    return lax.dot_general(a, b, (((1,), (1,)), ((), ())), preferred_element_type=F32)


def _sigmoid(x):
    return 0.5 * jnp.tanh(0.5 * x) + 0.5


def _silu(x):
    half = 0.5 * x
    return half * jnp.tanh(half) + half


def _softplus(x):
    return jnp.maximum(x, 0.0) + jnp.log(1.0 + jnp.exp(-jnp.abs(x)))


def _dot_mask_f32(x, mask_bf16):
    hi = x.astype(BF16)
    r1 = x - hi.astype(F32)
    mid = r1.astype(BF16)
    lo = (r1 - mid.astype(F32)).astype(BF16)
    return _dot(hi, mask_bf16) + _dot(mid, mask_bf16) + _dot(lo, mask_bf16)


def _ada_kernel(cond_t_ref, w_ref, b_ref, o_ref, *, nrows):
    s_t = _silu(cond_t_ref[...])
    w = w_ref[0]
    rows = [jnp.sum(s_t[:, r:r + 1] * w, axis=0, keepdims=True) for r in range(nrows)]
    rows.append(jnp.zeros((SUBLANES - nrows, w.shape[1]), F32))
    o_ref[0] = jnp.concatenate(rows, axis=0) + b_ref[0]


def _ada_all(cond_t, w_ada, b_ada, *, nrows):
    tn = 3 * D_MODEL
    return pl.pallas_call(
        functools.partial(_ada_kernel, nrows=nrows),
        grid=(DEPTH, 3 * D_MODEL // tn),
        in_specs=[
            pl.BlockSpec((D_MODEL, LANES), lambda l, j: (0, 0)),
            pl.BlockSpec((1, D_MODEL, tn), lambda l, j: (l, 0, j)),
            pl.BlockSpec((1, 1, tn), lambda l, j: (l, 0, j)),
        ],
        out_specs=pl.BlockSpec((1, SUBLANES, tn), lambda l, j: (l, 0, j)),
        out_shape=jax.ShapeDtypeStruct((DEPTH, SUBLANES, 3 * D_MODEL), F32),
        compiler_params=pltpu.CompilerParams(vmem_limit_bytes=VMEM_LIMIT),
        name="ada",
    )(cond_t, w_ada, b_ada.reshape(DEPTH, 1, 3 * D_MODEL))


def _inproj_kernel(*refs, full, tm, nblk):
    if full:
        (prev_ref, x_ref, next_ref, nw_ref, shift_ref, scale_ref, wa_ref, wb_ref, wcg_ref, wz_ref,
         wqkv_ref, wab_ref, alog_ref, dtb_ref, scw_ref,
         u_ref, cgs_ref, zs_ref, q_ref, k_ref, v_ref, gb_ref, h_scr, qkv_scr) = refs
    else:
        (prev_ref, x_ref, next_ref, nw_ref, shift_ref, scale_ref, wqkv_ref, wab_ref, alog_ref, dtb_ref,
         scw_ref, q_ref, k_ref, v_ref, gb_ref, h_scr, qkv_scr) = refs
    i = pl.program_id(1)
    gain = nw_ref[...] * (1.0 + scale_ref[0])

    def modulated(x):
        y = x * lax.rsqrt(jnp.mean(x * x, axis=-1, keepdims=True) + EPS)
        return (y * gain + shift_ref[0]).astype(BF16)

    h_scr[pl.ds(0, HALO), :] = modulated(prev_ref[0])
    h_scr[pl.ds(HALO, tm), :] = modulated(x_ref[0])
    h_scr[pl.ds(HALO + tm, HALO), :] = modulated(next_ref[0])

    def h_dot(w_ref):
        return _dot(h_scr[pl.ds(HALO, tm), :], w_ref[...])

    qkv = _dot(h_scr[...], wqkv_ref[...])
    for grp in range(3 * DN_HEADS):
        cols = slice(grp * LANES, (grp + 1) * LANES)
        qkv_scr[grp, pl.ds(0, HALO), :] = jnp.where(i > 0, qkv[:HALO, cols], 0.0)
        qkv_scr[grp, pl.ds(HALO, tm), :] = qkv[HALO:HALO + tm, cols]
        qkv_scr[grp, pl.ds(HALO + tm, HALO), :] = jnp.where(i < nblk - 1, qkv[HALO + tm:, cols], 0.0)

    def conv_silu(grp):
        acc = jnp.zeros((tm, LANES), F32)
        for j in range(SHORT_CONV_W):
            acc = acc + (qkv_scr[grp, pl.ds(HALO - SHORT_PAD + j, tm), :]
                         * scw_ref[j:j + 1, grp * LANES:(grp + 1) * LANES])
        return _silu(acc)

    def l2n(m):
        return m * lax.rsqrt(jnp.sum(m * m, axis=-1, keepdims=True) + EPS)

    def qkv_epilogue(hd):
        sl = slice(hd * LANES, (hd + 1) * LANES)
        q_ref[0, :, sl] = (l2n(conv_silu(hd)) * (DN_HEAD_DIM ** -0.5)).astype(q_ref.dtype)
        k_ref[0, :, sl] = l2n(conv_silu(DN_HEADS + hd)).astype(k_ref.dtype)
        v_ref[0, :, sl] = conv_silu(2 * DN_HEADS + hd).astype(v_ref.dtype)

    if full:
        u_ref[0] = h_dot(wa_ref) * _sigmoid(h_dot(wb_ref))
        qkv_epilogue(0)
        cgs_ref[0] = _silu(h_dot(wcg_ref)).astype(cgs_ref.dtype)
        qkv_epilogue(1)
        zs_ref[0] = _silu(h_dot(wz_ref)).astype(zs_ref.dtype)
        qkv_epilogue(2)
    else:
        for hd in range(DN_HEADS - 1):
            qkv_epilogue(hd)
    ab_t = h_dot(wab_ref).T
    g_t = -jnp.exp(alog_ref[...]) * _softplus(ab_t[0:2 * DN_HEADS] + dtb_ref[...])
    gb_ref[0] = jnp.concatenate([g_t, _sigmoid(ab_t[2 * DN_HEADS:4 * DN_HEADS])], axis=0)
    qkv_epilogue(DN_HEADS - 1)


def _inproj(x, nw, shift, scale, wts, alog, dtb, scw, *, full, tm):
    bsz, t, _ = x.shape
    nblk = t // tm
    halo_per_blk = tm // HALO
    last_halo = t // HALO - 1
    row = lambda n: pl.BlockSpec((1, tm, n), lambda b, i: (b, i, 0))
    const = lambda shp: pl.BlockSpec(shp, lambda b, i: (0,) * len(shp))
    per_b = pl.BlockSpec((1, 1, D_MODEL), lambda b, i: (b, 0, 0))
    halo_prev = pl.BlockSpec((1, HALO, D_MODEL),
                             lambda b, i: (b, jnp.maximum(i * halo_per_blk - 1, 0), 0))
    halo_next = pl.BlockSpec((1, HALO, D_MODEL),
                             lambda b, i: (b, jnp.minimum((i + 1) * halo_per_blk, last_halo), 0))
    w_specs = [pl.BlockSpec(w.shape, lambda b, i: (0, 0), pipeline_mode=pl.Buffered(1)) for w in wts]
    in_specs = ([halo_prev, row(D_MODEL), halo_next, const((1, D_MODEL)), per_b, per_b] + w_specs
                + [const((2 * DN_HEADS, 1))] * 2 + [const((SUBLANES, 3 * D_DN))])
    outs = ([(D_CONV, F32), (D_CONV, BF16), (D_DN, BF16)] if full else []) + [
        (D_DN, BF16), (D_DN, BF16), (D_DN, BF16)]
    gb_spec = pl.BlockSpec((1, 4 * DN_HEADS, tm), lambda b, i: (b, 0, i))
    gb_shape = jax.ShapeDtypeStruct((bsz, 4 * DN_HEADS, t), F32)
    return pl.pallas_call(
        functools.partial(_inproj_kernel, full=full, tm=tm, nblk=nblk),
        grid=(bsz, nblk),
        in_specs=in_specs,
        out_specs=[row(n) for n, _ in outs] + [gb_spec],
        out_shape=[jax.ShapeDtypeStruct((bsz, t, n), dt) for n, dt in outs] + [gb_shape],
        scratch_shapes=[pltpu.VMEM((tm + 2 * HALO, D_MODEL), BF16),
                        pltpu.VMEM((3 * DN_HEADS, tm + 2 * HALO, LANES), F32)],
        compiler_params=pltpu.CompilerParams(
            dimension_semantics=("parallel", "parallel"), vmem_limit_bytes=VMEM_LIMIT),
        name="inproj_full" if full else "inproj_dn",
    )(x, x, x, nw, shift, scale, *wts, alog, dtb, scw)


def _conv_segments(u_ref, w_ref, o_ref, pad_ref, *, seg, nseg):
    stride = seg + SEG_GAP
    zeros_gap = jnp.zeros((SEG_GAP, LANES), F32)

    def fill(r, carry):
        base = pl.multiple_of(r * stride, SUBLANES)
        src = pl.multiple_of(r * seg, SUBLANES)
        pad_ref[pl.ds(base, SEG_GAP), :] = zeros_gap
        pad_ref[pl.ds(base + SEG_GAP, seg), :] = u_ref[0, pl.ds(src, seg), :]
        return carry

    lax.fori_loop(0, nseg, fill, 0)
    pad_ref[pl.ds(nseg * stride, SEG_GAP), :] = zeros_gap

    def body(r, carry):
        base = pl.multiple_of(r * stride, SUBLANES)
        dst = pl.multiple_of(r * seg, BF16_ROWS)
        acc = jnp.zeros((seg, LANES), F32)
        for k in range(CONV_W):
            acc = acc + pad_ref[pl.ds(base + (SEG_GAP - CONV_PAD) + k, seg), :] * w_ref[k:k + 1, :]
        o_ref[0, pl.ds(dst, seg), :] = acc.astype(o_ref.dtype)
        return carry

    lax.fori_loop(0, nseg, body, 0, unroll=CONV_UNROLL if nseg % CONV_UNROLL == 0 else 1)


def _conv_strided(u_ref, w_ref, o_ref, pad_ref, *, t, step):
    halo = CONV_PAD * step
    pad_ref[pl.ds(0, halo), :] = jnp.zeros((halo, LANES), F32)
    pad_ref[pl.ds(halo + t, halo), :] = jnp.zeros((halo, LANES), F32)
    pad_ref[pl.ds(halo, t), :] = u_ref[0]

    def body(r, carry):
        base = pl.multiple_of(r * step, BF16_ROWS)
        acc = jnp.zeros((step, LANES), F32)
        for k in range(CONV_W):
            acc = acc + pad_ref[pl.ds(base + k * step, step), :] * w_ref[k:k + 1, :]
        o_ref[0, pl.ds(base, step), :] = acc.astype(o_ref.dtype)
        return carry

    lax.fori_loop(0, t // step, body, 0, unroll=CONV_UNROLL if (t // step) % CONV_UNROLL == 0 else 1)


def _grid_conv_kernel(u_ref, w_ref, o_ref, pad_ref, *, t):
    j = pl.program_id(1)
    half_blocks = D_CONV // LANES // 2

    @pl.when(j < half_blocks)
    def _():
        _conv_segments(u_ref, w_ref, o_ref, pad_ref, seg=GRID_W, nseg=t // GRID_W)

    @pl.when(j >= half_blocks)
    def _():
        _conv_strided(u_ref, w_ref, o_ref, pad_ref, t=t, step=GRID_W)


def _seq_conv_kernel(u_ref, w_ref, o_ref, pad_ref, *, t):
    _conv_segments(u_ref, w_ref, o_ref, pad_ref, seg=t, nseg=1)


def _dwconv(u, w_pad, *, grid_mode):
    bsz, t, _ = u.shape
    if grid_mode:
        rows = t // GRID_W
        pad_rows = max(rows * (GRID_W + SEG_GAP) + SEG_GAP, t + 2 * CONV_PAD * GRID_W)
        body = functools.partial(_grid_conv_kernel, t=t)
    else:
        pad_rows = t + 2 * SEG_GAP
        body = functools.partial(_seq_conv_kernel, t=t)
    return pl.pallas_call(
        body,
        grid=(bsz, D_CONV // LANES),
        in_specs=[
            pl.BlockSpec((1, t, LANES), lambda b, j: (b, 0, j)),
            pl.BlockSpec((4 * SUBLANES, LANES), lambda b, j: (0, j)),
        ],
        out_specs=pl.BlockSpec((1, t, LANES), lambda b, j: (b, 0, j)),
        out_shape=jax.ShapeDtypeStruct((bsz, t, D_CONV), BF16),
        scratch_shapes=[pltpu.VMEM((pad_rows, LANES), F32)],
        compiler_params=pltpu.CompilerParams(
            dimension_semantics=("parallel", "parallel"), vmem_limit_bytes=VMEM_LIMIT),
        name="grid_conv" if grid_mode else "seq_conv",
    )(u, w_pad)


def _delta_kernel(*refs, reverse, nblk, nb, add_other):
    if add_other:
        (q_ref, k_ref, v_ref, gb_ref, s0_ref, other_ref, o_ref, sfin_ref,
         s_scr, u_scr, wq_scr, ak_scr, gl_scr) = refs
    else:
        (q_ref, k_ref, v_ref, gb_ref, s0_ref, o_ref, sfin_ref,
         s_scr, u_scr, wq_scr, ak_scr, gl_scr) = refs
    i = pl.program_id(1)
    heads = range(DN_HEADS)
    nchunk = nb // CHUNK

    @pl.when(i == 0)
    def _():
        s_scr[...] = s0_ref[0]
        u_scr[...] = jnp.zeros(u_scr.shape, F32)
        wq_scr[...] = jnp.zeros(wq_scr.shape, BF16)
        ak_scr[...] = jnp.zeros(ak_scr.shape, BF16)
        gl_scr[...] = jnp.ones(gl_scr.shape, F32)

    def recurrence():
        s = [s_scr[h] for h in heads]
        for step in range(nchunk):
            c = (nchunk - 1 - step) if reverse else step
            r = [_dot(wq_scr[h, c], s[h].astype(BF16)) for h in heads]
            yield
            v_new = [(u_scr[h, pl.ds(c * CHUNK, CHUNK), :] - r[h][:CHUNK]).astype(BF16) for h in heads]
            av = [_dot(ak_scr[h, c], v_new[h]) for h in heads]
            for h in heads:
                o_c = av[h][:CHUNK] + r[h][CHUNK:]
                if add_other:
                    o_c = o_c + other_ref[0, pl.ds(c * CHUNK, CHUNK), h * LANES:(h + 1) * LANES].astype(F32)
                o_ref[0, pl.ds(c * CHUNK, CHUNK), h * LANES:(h + 1) * LANES] = o_c.astype(o_ref.dtype)
            s = [av[h][CHUNK:]
                 + s[h] * jnp.concatenate([gl_scr[h, c]] * (DN_HEAD_DIM // SUBLANES), axis=0)
                 for h in heads]
            yield
        for h in heads:
            s_scr[h] = s[h]
        while True:
            yield

    rec = recurrence()
    half_steps_per_stage = -(-(2 * nchunk + 1) // 10)

    def advance():
        for _ in range(half_steps_per_stage):
            next(rec)

    advance()

    sb = SOLVE_BLOCK
    units = [(h, hf) for h in heads for hf in range(nb // sb)]
    nu = range(len(units))
    lgs = [(DN_HEADS if reverse else 0) + h for h, _ in units]
    tok = [slice(hf * sb, (hf + 1) * sb) for _, hf in units]
    qbf = [q_ref[0, tok[u], h * LANES:(h + 1) * LANES] for u, (h, _) in enumerate(units)]
    kbf = [k_ref[0, tok[u], h * LANES:(h + 1) * LANES] for u, (h, _) in enumerate(units)]

    def chunk_masks(n_tok):
        row = lax.broadcasted_iota(jnp.int32, (n_tok, n_tok), 0)
        col = lax.broadcasted_iota(jnp.int32, (n_tok, n_tok), 1)
        same = (row // CHUNK) == (col // CHUNK)
        before = (col > row) if reverse else (col < row)
        return row, col, same, before

    _, _, same_nb, before_nb = chunk_masks(nb)
    gbt = gb_ref[0]
    nrow = 4 * DN_HEADS
    cum_mask = jnp.logical_and(same_nb, jnp.logical_not(before_nb))
    gc_t = _dot_mask_f32(gbt, jnp.where(cum_mask, 1.0, 0.0).astype(BF16))
    tot_t = _dot_mask_f32(gbt, jnp.where(same_nb, 1.0, 0.0).astype(BF16))
    cols = jnp.concatenate([gbt, gc_t, tot_t, jnp.zeros((LANES - 3 * nrow, nb), F32)], axis=0).T
    advance()

    row, col, same, before = chunk_masks(sb)
    strict = jnp.logical_and(same, before)
    incl = jnp.logical_or(strict, row == col)
    base_blocks = (row // INV_BASE) == (col // INV_BASE)

    q = [qbf[u].astype(F32) for u in nu]
    k = [kbf[u].astype(F32) for u in nu]
    v = [v_ref[0, tok[u], h * LANES:(h + 1) * LANES].astype(F32) for u, (h, _) in enumerate(units)]
    beta = [cols[tok[u], 2 * DN_HEADS + lgs[u]:2 * DN_HEADS + lgs[u] + 1] for u in nu]
    gcol = [cols[tok[u], nrow + lgs[u]:nrow + lgs[u] + 1] for u in nu]
    tcol = [cols[tok[u], 2 * nrow + lgs[u]:2 * nrow + lgs[u] + 1] for u in nu]
    decay = [jnp.where(incl, jnp.exp(jnp.where(incl, gcol[u] - gc_t[lgs[u]:lgs[u] + 1, tok[u]], 0.0)), 0.0)
             for u in nu]
    kb = [k[u] * beta[u] for u in nu]
    eg = [jnp.exp(gcol[u]) for u in nu]
    lfull = [_dot_nt(kb[u].astype(BF16), kbf[u]) * jnp.where(strict, decay[u], 0.0) for u in nu]
    advance()
    l0 = [jnp.where(base_blocks, lfull[u], 0.0) for u in nu]
    pb = [l0[u].astype(BF16) for u in nu]
    n = [-l0[u] for u in nu]
    nbf = [(-l0[u]).astype(BF16) for u in nu]
    doublings = INV_BASE.bit_length() - 2
    for it in range(doublings):
        pw = [_dot(pb[u], pb[u]) for u in nu]
        n_plus = [n[u] + pw[u] for u in nu]
        pb = [pw[u].astype(BF16) for u in nu]
        advance()
        n = [_dot(nbf[u], pb[u]) + n_plus[u] for u in nu]
        nbf = [n[u].astype(BF16) for u in nu]
        advance()
    size = INV_BASE
    while size < CHUNK:
        picked = [j for j in range(sb // size) if j % 2 == (0 if reverse else 1)]

        def take(m):
            return jnp.concatenate([m[j * size:(j + 1) * size] for j in picked], axis=0)

        def put(full_blocks, picked_rows):
            blocks = list(full_blocks)
            for idx, j in enumerate(picked):
                blocks[j] = picked_rows[idx * size:(idx + 1) * size]
            return jnp.concatenate(blocks, axis=0)

        prow = lax.broadcasted_iota(jnp.int32, (sb // 2, sb), 0)
        pcol = lax.broadcasted_iota(jnp.int32, (sb // 2, sb), 1)
        orig = (prow // size) * (2 * size) + (0 if reverse else size) + prow % size
        pair_mask = jnp.logical_and(orig // (2 * size) == pcol // (2 * size), orig // size != pcol // size)
        zero_blocks = [jnp.zeros((size, sb), BF16)] * (sb // size)
        cm = [jnp.where(pair_mask, take(lfull[u]), 0.0) for u in nu]
        nbf = [n[u].astype(BF16) for u in nu]
        xm = [_dot(cm[u].astype(BF16), nbf[u]) + cm[u] for u in nu]
        advance()
        ym = [_dot(take(nbf[u]), put(zero_blocks, xm[u].astype(BF16))) + xm[u] for u in nu]
        n = [put([n[u][j * size:(j + 1) * size] for j in range(sb // size)], take(n[u]) - ym[u])
             for u in nu]
        advance()
        size *= 2
    rhs = [jnp.concatenate([v[u] * beta[u], kb[u] * eg[u]], axis=-1) for u in nu]
    y = [_dot(n[u].astype(BF16), rhs[u].astype(BF16)) + rhs[u] for u in nu]
    advance()
    attn = [_dot_nt(qbf[u], kbf[u]) * decay[u] for u in nu]
    for _ in range(2 * nchunk + 1):
        next(rec)
    for u, (h, hf) in enumerate(units):
        kt_t = (k[u] * jnp.exp(tcol[u] - gcol[u])).T
        qg = q[u] * eg[u]
        u_scr[h, pl.ds(hf * sb, sb), :] = y[u][:, :LANES]
        for cc in range(sb // CHUNK):
            c = hf * (sb // CHUNK) + cc
            sl = slice(cc * CHUNK, (cc + 1) * CHUNK)
            wq_scr[h, c, 0:CHUNK, :] = y[u][sl, LANES:].astype(BF16)
            wq_scr[h, c, CHUNK:2 * CHUNK, :] = qg[sl, :].astype(BF16)
            ak_scr[h, c, 0:CHUNK, :] = attn[u][sl, sl].astype(BF16)
            ak_scr[h, c, CHUNK:, :] = kt_t[:, sl].astype(BF16)
            gl_scr[h, c] = jnp.broadcast_to(
                jnp.exp(tot_t[lgs[u]:lgs[u] + 1, c * CHUNK:c * CHUNK + 1]), (SUBLANES, LANES))

    @pl.when(i == nblk)
    def _():
        sfin_ref[0] = s_scr[...]


def _delta(q, k, v, gb, s0, *, reverse, other=None):
    bsz, t, _ = q.shape
    nb = min(DELTA_BLOCK, t)
    nblk = t // nb
    nchunk = nb // CHUNK
    if reverse:
        bidx = lambda i: nblk - 1 - i
    else:
        bidx = lambda i: i
    blk = lambda n: pl.BlockSpec((1, nb, n), lambda b, i: (b, bidx(jnp.minimum(i, nblk - 1)), 0))
    gb_blk = pl.BlockSpec((1, 4 * DN_HEADS, nb), lambda b, i: (b, 0, bidx(jnp.minimum(i, nblk - 1))))
    out_blk = pl.BlockSpec((1, nb, D_DN), lambda b, i: (b, bidx(jnp.maximum(i - 1, 0)), 0))
    state_spec = pl.BlockSpec((1, DN_HEADS, DN_HEAD_DIM, DN_HEAD_DIM), lambda b, i: (b, 0, 0, 0))
    in_specs = [blk(D_DN), blk(D_DN), blk(D_DN), gb_blk, state_spec]
    args = [q, k, v, gb, s0]
    if other is not None:
        in_specs.append(out_blk)
        args.append(other)
    return pl.pallas_call(
        functools.partial(_delta_kernel, reverse=reverse, nblk=nblk, nb=nb, add_other=other is not None),
        grid=(bsz, nblk + 1),
        in_specs=in_specs,
        out_specs=[out_blk, state_spec],
        out_shape=[
            jax.ShapeDtypeStruct((bsz, t, D_DN), BF16),
            jax.ShapeDtypeStruct((bsz, DN_HEADS, DN_HEAD_DIM, DN_HEAD_DIM), F32),
        ],
        scratch_shapes=[
            pltpu.VMEM((DN_HEADS, DN_HEAD_DIM, DN_HEAD_DIM), F32),
            pltpu.VMEM((DN_HEADS, nb, LANES), F32),
            pltpu.VMEM((DN_HEADS, nchunk, 2 * CHUNK, LANES), BF16),
            pltpu.VMEM((DN_HEADS, nchunk, CHUNK + DN_HEAD_DIM, CHUNK), BF16),
            pltpu.VMEM((DN_HEADS, nchunk, SUBLANES, LANES), F32),
        ],
        compiler_params=pltpu.CompilerParams(
            dimension_semantics=("arbitrary", "arbitrary"), vmem_limit_bytes=VMEM_LIMIT),
        name="delta_bwd" if reverse else "delta_fwd",
    )(*args)


def _outproj_kernel(*refs, final):
    if final:
        (conv_ref, cgs_ref, odn_ref, zs_ref, x_ref, cb_ref, lnw_ref, lnb_ref, dnw_ref,
         gate_ref, wc_ref, wd_ref, fnw_ref, o_ref) = refs
    else:
        (conv_ref, cgs_ref, odn_ref, zs_ref, x_ref, cb_ref, lnw_ref, lnb_ref, dnw_ref,
         gate_ref, wc_ref, wd_ref, o_ref) = refs
    yc = conv_ref[0].astype(F32) + cb_ref[...]
    mu = jnp.mean(yc, axis=-1, keepdims=True)
    d = yc - mu
    var = jnp.mean(d * d, axis=-1, keepdims=True)
    yn = d * lax.rsqrt(var + LN_EPS) * lnw_ref[...] + lnb_ref[...]
    y_conv = (_silu(yn) * cgs_ref[0].astype(F32)).astype(BF16)
    o = odn_ref[0].astype(F32)
    y_dn = []
    for h in range(DN_HEADS):
        sl = slice(h * LANES, (h + 1) * LANES)
        oh = o[:, sl]
        on = oh * lax.rsqrt(jnp.mean(oh * oh, axis=-1, keepdims=True) + EPS) * dnw_ref[...]
        y_dn.append((on * zs_ref[0, :, sl].astype(F32)).astype(BF16))
    acc = _dot(y_conv, wc_ref[...]) + _dot(jnp.concatenate(y_dn, axis=-1), wd_ref[...])
    xn = x_ref[0] + gate_ref[0] * acc
    if final:
        xn = xn * lax.rsqrt(jnp.mean(xn * xn, axis=-1, keepdims=True) + EPS) * fnw_ref[...]
    o_ref[0] = xn


def _outproj(conv, cgs, o_dn, zs, x, cb, lnw, lnb, dnw, gate, wc, wd, fnw, *, tm):
    bsz, t, _ = x.shape
    final = fnw is not None
    row = lambda n: pl.BlockSpec((1, tm, n), lambda b, i: (b, i, 0))
    const = lambda shp: pl.BlockSpec(shp, lambda b, i: (0,) * len(shp))
    per_b = pl.BlockSpec((1, 1, D_MODEL), lambda b, i: (b, 0, 0))
    in_specs = [row(D_CONV), row(D_CONV), row(D_DN), row(D_DN), row(D_MODEL),
                const((1, D_CONV)), const((1, D_CONV)), const((1, D_CONV)), const((1, DN_HEAD_DIM)),
                per_b, const((D_CONV, D_MODEL)), const((D_DN, D_MODEL))]
    args = [conv, cgs, o_dn, zs, x, cb, lnw, lnb, dnw, gate, wc, wd]
    if final:
        in_specs.append(const((1, D_MODEL)))
        args.append(fnw)
    return pl.pallas_call(
        functools.partial(_outproj_kernel, final=final),
        grid=(bsz, t // tm),
        in_specs=in_specs,
        out_specs=row(D_MODEL),
        out_shape=jax.ShapeDtypeStruct((bsz, t, D_MODEL), F32),
        compiler_params=pltpu.CompilerParams(
            dimension_semantics=("parallel", "parallel"), vmem_limit_bytes=VMEM_LIMIT),
        name="outproj_final" if final else "outproj",
    )(*args)


def _layer_weights(l, norm_w, w_in, conv_w, conv_b, conv_ln_w, conv_ln_b, short_conv_w, a_log,
                   dt_bias, dn_norm_w, w_out):
    wi = w_in[l]
    c0 = 3 * D_CONV
    w_ab = jnp.zeros((D_MODEL, LANES), F32).at[:, :4 * DN_HEADS].set(wi[:, c0 + 4 * D_DN:])
    pad_vec = lambda v: v.reshape(2 * DN_HEADS, 1)
    return dict(
        nw=norm_w[l].reshape(1, D_MODEL),
        w_full=[wi[:, :D_CONV].astype(BF16), wi[:, D_CONV:2 * D_CONV].astype(BF16),
                wi[:, 2 * D_CONV:c0].astype(BF16), wi[:, c0 + 3 * D_DN:c0 + 4 * D_DN].astype(BF16)],
        w_dn=[wi[:, c0:c0 + 3 * D_DN].astype(BF16), w_ab.astype(BF16)],
        alog=pad_vec(a_log[l]), dtb=pad_vec(dt_bias[l]),
        conv_w=jnp.zeros((4 * SUBLANES, D_CONV), F32).at[:CONV_W].set(conv_w[l]),
        conv_b=conv_b[l].reshape(1, D_CONV), lnw=conv_ln_w[l].reshape(1, D_CONV),
        lnb=conv_ln_b[l].reshape(1, D_CONV),
        scw=jnp.zeros((SUBLANES, 3 * D_DN), F32).at[:SHORT_CONV_W].set(short_conv_w[l]),
        dnw=dn_norm_w[l].reshape(1, DN_HEAD_DIM),
        wc=w_out[l, :D_CONV].astype(BF16), wd=w_out[l, D_CONV:].astype(BF16),
    )


def _mixer(x, mod, p, s0_f, s0_b, *, grid_mode, tm, fnw=None, states_only=False):
    bsz = x.shape[0]
    shift, scale, gate = (mod[:, j * D_MODEL:(j + 1) * D_MODEL].reshape(bsz, 1, D_MODEL) for j in range(3))
    if states_only:
        q, k, v, gb = _inproj(x, p["nw"], shift, scale, p["w_dn"], p["alog"], p["dtb"], p["scw"],
                              full=False, tm=tm)
        _, s_f = _delta(q, k, v, gb, s0_f, reverse=False)
        _, s_b = _delta(q, k, v, gb, s0_b, reverse=True)
        return x, s_f, s_b
    u, cgs, zs, q, k, v, gb = _inproj(x, p["nw"], shift, scale, p["w_full"] + p["w_dn"], p["alog"],
                                      p["dtb"], p["scw"], full=True, tm=tm)
    conv = _dwconv(u, p["conv_w"], grid_mode=grid_mode)
    o_f, s_f = _delta(q, k, v, gb, s0_f, reverse=False)
    o_dn, s_b = _delta(q, k, v, gb, s0_b, reverse=True, other=o_f)
    x_new = _outproj(conv, cgs, o_dn, zs, x, p["conv_b"], p["lnw"], p["lnb"], p["dnw"], gate,
                     p["wc"], p["wd"], fnw, tm=min(x.shape[1], OUT_TILE))
    return x_new, s_f, s_b


def kernel(x, c, ctx, c_ctx, norm_w, w_ada, b_ada, w_in, conv_w, conv_b, conv_ln_w, conv_ln_b, short_conv_w, a_log, dt_bias, dn_norm_w, w_out, final_norm_w):
    bsz = x.shape[0]
    cond_t = jnp.zeros((D_MODEL, LANES), F32).at[:, :bsz].set(c.T).at[:, bsz].set(c_ctx)
    mods = _ada_all(cond_t, w_ada, b_ada, nrows=bsz + 1)
    zeros = jnp.zeros((bsz, DN_HEADS, DN_HEAD_DIM, DN_HEAD_DIM), F32)
    for l in range(DEPTH):
        p = _layer_weights(l, norm_w, w_in, conv_w, conv_b, conv_ln_w, conv_ln_b, short_conv_w,
                           a_log, dt_bias, dn_norm_w, w_out)
        mod_ctx = jnp.broadcast_to(mods[l, bsz:bsz + 1], (bsz, 3 * D_MODEL))
        last = l == DEPTH - 1
        ctx_new, s_f, s_b = _mixer(ctx, mod_ctx, p, zeros, zeros, grid_mode=False, tm=ctx.shape[1],
                                   states_only=last)
        x, _, _ = _mixer(x, mods[l, :bsz], p, s_f, s_b, grid_mode=True, tm=IN_TILE,
                         fnw=final_norm_w.reshape(1, D_MODEL) if last else None)
        ctx = ctx_new
    return x
```
